```python
import jax
import jax.numpy as jnp
from jax import lax
import numpy as np

D_MODEL = 4096
BATCH = 4
SEQ = 2048
DEPTH = 1

GRID_W = 64
CTX_LEN = 256
EPS = 1e-6
N_MOD = 6
RET_HEADS = D_MODEL // 256
RET_QK_DIM = 128
RET_V_DIM = 256
RET_QK_W = RET_HEADS * RET_QK_DIM
RET_V_W = RET_HEADS * RET_V_DIM
RET_CHUNK = 128
ROPE_BASE = 10000.0
LRU_WIDTH = D_MODEL
LRU_BLOCKS = 16
LRU_BLOCK = LRU_WIDTH // LRU_BLOCKS
LRU_C = 8.0
LRU_CONV_W = 4
LRU_CONV_PAD = (2, 1)
FFN_DIM = ((8 * D_MODEL // 3 + 255) // 256) * 256
FFN_CONV_W = 3
FFN_CONV_PAD = (1, 1)
COL_SIZES = (RET_QK_W, RET_V_W, LRU_WIDTH, RET_QK_W, RET_V_W, LRU_WIDTH, D_MODEL, D_MODEL)
STATE_COLS = RET_QK_W + RET_V_W + LRU_WIDTH
IN_COLS = sum(COL_SIZES)
SPLIT_POINTS = tuple(int(s) for s in np.cumsum(COL_SIZES)[:-1])
STATE_SPLIT_POINTS = (RET_QK_W, RET_QK_W + RET_V_W)

kernel_name = 'hybrid_retention_rglru_convffn_dit_layer'


def rms_norm(x, gain):
    x32 = x.astype(jnp.float32)
    y = x32 * lax.rsqrt(jnp.mean(x32 * x32, axis=-1, keepdims=True) + EPS)
    return y.astype(x.dtype) * gain


def modulate(h, shift, scale):
    return h * (1 + scale) + shift


def depthwise_conv(x, w, b, pad):
    y = lax.conv_general_dilated(x, w[:, None, :], window_strides=(1,), padding=[pad],
                                 dimension_numbers=('NWC', 'WIO', 'NWC'),
                                 feature_group_count=x.shape[-1])
    return y + b


def rope_2d(n_tokens):
    rows = n_tokens // GRID_W
    row_ids = jnp.repeat(jnp.arange(rows, dtype=jnp.float32), GRID_W)
    col_ids = jnp.tile(jnp.arange(GRID_W, dtype=jnp.float32), rows)
    n_freq = RET_QK_DIM // 4
    inv_freq = ROPE_BASE ** (-jnp.arange(n_freq, dtype=jnp.float32) / n_freq)
    ang = jnp.concatenate([row_ids[:, None] * inv_freq, col_ids[:, None] * inv_freq], axis=-1)
    return jnp.cos(ang), jnp.sin(ang)


def apply_rope(t, rope):
    cos, sin = rope
    cos = cos.astype(t.dtype)
    sin = sin.astype(t.dtype)
    t1, t2 = jnp.split(t, 2, axis=-1)
    return jnp.concatenate([t1 * cos - t2 * sin, t1 * sin + t2 * cos], axis=-1)


def to_heads(t, head_dim):
    bsz, n, _ = t.shape
    return t.reshape(bsz, n, -1, head_dim).transpose(0, 2, 1, 3)


def both_directions(t, axis):
    return jnp.stack([t, jnp.flip(t, axis)], axis=0)


def retention_kv(k, v, rope):
    k = to_heads(k, RET_QK_DIM)
    if rope is not None:
        k = apply_rope(k, rope)
    k = k * (RET_QK_DIM ** -0.5)
    return both_directions(k, 2), both_directions(to_heads(v, RET_V_DIM), 2)


def retention_scan(q, k, v, log_gamma, s0):
    z, bsz, nh, t, _ = q.shape
    dv = v.shape[-1]
    n_chunks = t // RET_CHUNK
    pos = jnp.arange(RET_CHUNK, dtype=jnp.float32)
    lg = log_gamma[:, None, :, None]
    rel = pos[:, None] - pos[None, :]
    decay_mat = jnp.where(rel >= 0, jnp.exp(jnp.maximum(rel, 0.0) * lg[..., None]), 0.0)
    q_decay = jnp.exp((pos + 1.0) * lg)[..., None]
    k_decay = jnp.exp((RET_CHUNK - 1.0 - pos) * lg)[..., None]
    chunk_decay = jnp.exp(RET_CHUNK * lg)[..., None]

    def chunks(a):
        a = a.astype(jnp.float32).reshape(z, bsz, nh, n_chunks, RET_CHUNK, a.shape[-1])
        return jnp.moveaxis(a, 3, 0)

    def step(s, inp):
        qc, kc, vc = inp
        scores = jnp.einsum('zbhid,zbhjd->zbhij', qc, kc) * decay_mat
        o = (jnp.einsum('zbhij,zbhje->zbhie', scores, vc)
             + jnp.einsum('zbhid,zbhde->zbhie', qc * q_decay, s))
        s = chunk_decay * s + jnp.einsum('zbhjd,zbhje->zbhde', kc * k_decay, vc)
        return s, o

    s_final, o = lax.scan(step, s0, (chunks(q), chunks(k), chunks(v)))
    o = jnp.moveaxis(o, 0, 3).reshape(z, bsz, nh, t, dv)
    return o, s_final


def retention_final_state(k, v, log_gamma):
    n = k.shape[3]
    pos = jnp.arange(n, dtype=jnp.float32)
    w = jnp.exp((n - 1.0 - pos) * log_gamma[:, None, :, None])[..., None]
    return jnp.einsum('zbhjd,zbhje->zbhde', k.astype(jnp.float32) * w, v.astype(jnp.float32))


def head_norm(o):
    mu = jnp.mean(o, axis=-1, keepdims=True)
    var = jnp.mean(jnp.square(o - mu), axis=-1, keepdims=True)
    return (o - mu) * lax.rsqrt(var + EPS)


def lru_inputs(xl, lp):
    xc = depthwise_conv(xl, lp['lru_conv_w'], lp['lru_conv_b'], LRU_CONV_PAD)
    bsz, t, w = xc.shape
    xb = xc.reshape(bsz, t, LRU_BLOCKS, LRU_BLOCK)
    gate_r = jnp.einsum('btnd,zndf->zbtnf', xb, lp['lru_wa']).reshape(2, bsz, t, w) + lp['lru_ba'][:, None, None, :]
    gate_i = jnp.einsum('btnd,zndf->zbtnf', xb, lp['lru_wx']).reshape(2, bsz, t, w) + lp['lru_bx'][:, None, None, :]
    r = jax.nn.sigmoid(gate_r.astype(jnp.float32))
    i = jax.nn.sigmoid(gate_i.astype(jnp.float32))
    log_a = -LRU_C * jax.nn.softplus(-lp['lru_lambda'].astype(jnp.float32))[:, None, None, :] * r
    a = jnp.exp(log_a)
    b = jnp.sqrt(-jnp.expm1(2.0 * log_a)) * i * xc.astype(jnp.float32)[None]
    a = jnp.stack([a[0], jnp.flip(a[1], 1)])
    b = jnp.stack([b[0], jnp.flip(b[1], 1)])
    return a, b


def linear_scan(a, b, h0):
    def combine(left, right):
        return left[0] * right[0], right[0] * left[1] + right[1]
    a_cum, h = lax.associative_scan(combine, (a, b), axis=2)
    return a_cum * h0[:, :, None, :] + h


def mixer(h, lp, ret_s0, lru_h0, rope):
    k, v, xl, q, g_ret, g_lru, m_ret, m_lru = jnp.split(h @ lp['w_in'], SPLIT_POINTS, axis=-1)
    bsz, n, _ = h.shape
    q = to_heads(q, RET_QK_DIM)
    if rope is not None:
        q = apply_rope(q, rope)
    k2, v2 = retention_kv(k, v, rope)
    o, ret_sf = retention_scan(both_directions(q, 2), k2, v2, lp['log_gamma'], ret_s0)
    o = head_norm(o[0] + jnp.flip(o[1], 2))
    o = o.transpose(0, 2, 1, 3).reshape(bsz, n, RET_V_W).astype(h.dtype)
    y_ret = (jax.nn.silu(g_ret) * o) @ lp['w_ret_o']
    a, b = lru_inputs(xl, lp)
    hs = linear_scan(a, b, lru_h0)
    lru_hf = hs[:, :, -1]
    y_l = (hs[0] + jnp.flip(hs[1], 1)).astype(h.dtype)
    y_lru = (jax.nn.gelu(g_lru) * y_l) @ lp['w_lru_o']
    y = jax.nn.sigmoid(m_ret) * y_ret + jax.nn.sigmoid(m_lru) * y_lru
    return y @ lp['w_out'], ret_sf, lru_hf


def context_states(h, lp):
    k, v, xl = jnp.split(h @ lp['w_in'][:, :STATE_COLS], STATE_SPLIT_POINTS, axis=-1)
    k2, v2 = retention_kv(k, v, None)
    ret_s = retention_final_state(k2, v2, lp['log_gamma'])
    a, b = lru_inputs(xl, lp)
    h0 = jnp.zeros((2, h.shape[0], LRU_WIDTH), jnp.float32)
    lru_h = linear_scan(a, b, h0)[:, :, -1]
    return ret_s, lru_h


def conv_ffn(h, lp):
    u = depthwise_conv(h @ lp['w_up'], lp['ffn_conv_w'], lp['ffn_conv_b'], FFN_CONV_PAD)
    g, v = jnp.split(u, 2, axis=-1)
    return (jax.nn.silu(g) * v) @ lp['w_down']


def setup_inputs(seed: int = 0) -> dict:
    key = jax.random.key(seed)
    ks = jax.random.split(key, 25)
    f32 = jnp.float32
    D = D_MODEL

    def nrm(k, shape, scale):
        return jax.random.normal(k, shape, f32) * scale

    h_idx = np.arange(RET_HEADS, dtype=np.float32)
    gamma0 = 1.0 - np.float32(2.0) ** (-5.0 - h_idx)
    logit0 = jnp.asarray(np.log(gamma0) - np.log1p(-gamma0), f32)
    a0 = jax.random.uniform(ks[16], (DEPTH, 2, LRU_WIDTH), f32, 0.9, 0.999)
    s0 = a0 ** (1.0 / LRU_C)
    lam = jnp.log(s0) - jnp.log1p(-s0)
    return {
        'x': nrm(ks[0], (BATCH, SEQ, D), 1.0),
        'c': nrm(ks[1], (BATCH, D), 1.0),
        'ctx': nrm(ks[2], (BATCH, CTX_LEN, D), 1.0),
        'c_ctx': nrm(ks[3], (D,), 1.0),
        'w_ada': nrm(ks[4], (DEPTH, D, N_MOD * D), 0.5 * D ** -0.5),
        'b_ada': nrm(ks[5], (DEPTH, N_MOD * D), 0.01),
        'norm1': 1.0 + nrm(ks[6], (DEPTH, D), 0.02),
        'norm2': 1.0 + nrm(ks[7], (DEPTH, D), 0.02),
        'w_in': nrm(ks[8], (DEPTH, D, IN_COLS), D ** -0.5),
        'ret_decay_logit': logit0[None, None, :] + nrm(ks[9], (DEPTH, 2, RET_HEADS), 0.1),
        'lru_conv_w': nrm(ks[10], (DEPTH, LRU_CONV_W, LRU_WIDTH), LRU_CONV_W ** -0.5),
        'lru_conv_b': nrm(ks[11], (DEPTH, LRU_WIDTH), 0.01),
        'lru_wa': nrm(ks[12], (DEPTH, 2, LRU_BLOCKS, LRU_BLOCK, LRU_BLOCK), LRU_BLOCK ** -0.5),
        'lru_ba': nrm(ks[13], (DEPTH, 2, LRU_WIDTH), 0.01),
        'lru_wx': nrm(ks[14], (DEPTH, 2, LRU_BLOCKS, LRU_BLOCK, LRU_BLOCK), LRU_BLOCK ** -0.5),
        'lru_bx': nrm(ks[15], (DEPTH, 2, LRU_WIDTH), 0.01),
        'lru_lambda': lam,
        'w_ret_o': nrm(ks[17], (DEPTH, RET_V_W, D), RET_V_W ** -0.5),
        'w_lru_o': nrm(ks[18], (DEPTH, LRU_WIDTH, D), LRU_WIDTH ** -0.5),
        'w_out': nrm(ks[19], (DEPTH, D, D), D ** -0.5),
        'w_up': nrm(ks[20], (DEPTH, D, 2 * FFN_DIM), D ** -0.5),
        'ffn_conv_w': nrm(ks[21], (DEPTH, FFN_CONV_W, 2 * FFN_DIM), FFN_CONV_W ** -0.5),
        'ffn_conv_b': nrm(ks[22], (DEPTH, 2 * FFN_DIM), 0.01),
        'w_down': nrm(ks[23], (DEPTH, FFN_DIM, D), FFN_DIM ** -0.5),
        'final_norm': 1.0 + nrm(ks[24], (D,), 0.02),
    }


def reference(x, c, ctx, c_ctx, w_ada, b_ada, norm1, norm2, w_in, ret_decay_logit,
              lru_conv_w, lru_conv_b, lru_wa, lru_ba, lru_wx, lru_bx, lru_lambda,
              w_ret_o, w_lru_o, w_out, w_up, ffn_conv_w, ffn_conv_b, w_down, final_norm):
    bsz = x.shape[0]
    rope = rope_2d(x.shape[1])
    silu_c = jax.nn.silu(c)
    silu_cc = jax.nn.silu(c_ctx)
    x_lat, x_ctx = x, ctx
    for l in range(DEPTH):
        last = l == DEPTH - 1
        lp = {
            'w_in': w_in[l],
            'log_gamma': jax.nn.log_sigmoid(ret_decay_logit[l].astype(jnp.float32)),
            'lru_conv_w': lru_conv_w[l], 'lru_conv_b': lru_conv_b[l],
            'lru_wa': lru_wa[l], 'lru_ba': lru_ba[l],
            'lru_wx': lru_wx[l], 'lru_bx': lru_bx[l], 'lru_lambda': lru_lambda[l],
            'w_ret_o': w_ret_o[l], 'w_lru_o': w_lru_o[l], 'w_out': w_out[l],
            'w_up': w_up[l], 'ffn_conv_w': ffn_conv_w[l], 'ffn_conv_b': ffn_conv_b[l],
            'w_down': w_down[l],
        }
        mod = silu_c @ w_ada[l] + b_ada[l]
        sh1, sc1, g1, sh2, sc2, g2 = jnp.split(mod[:, None, :], N_MOD, axis=-1)
        n_ctx_mod = 2 if last else N_MOD
        cmod = jnp.split(silu_cc @ w_ada[l][:, :n_ctx_mod * D_MODEL] + b_ada[l][:n_ctx_mod * D_MODEL], n_ctx_mod)
        h_ctx = modulate(rms_norm(x_ctx, norm1[l]), cmod[0], cmod[1])
        if last:
            ret_s, lru_h = context_states(h_ctx, lp)
        else:
            ret_zero = jnp.zeros((2, bsz, RET_HEADS, RET_QK_DIM, RET_V_DIM), jnp.float32)
            lru_zero = jnp.zeros((2, bsz, LRU_WIDTH), jnp.float32)
            y_ctx, ret_s, lru_h = mixer(h_ctx, lp, ret_zero, lru_zero, None)
            x_ctx = x_ctx + cmod[2] * y_ctx
            h2c = modulate(rms_norm(x_ctx, norm2[l]), cmod[3], cmod[4])
            x_ctx = x_ctx + cmod[5] * conv_ffn(h2c, lp)
        h_lat = modulate(rms_norm(x_lat, norm1[l]), sh1, sc1)
        y_lat, _, _ = mixer(h_lat, lp, ret_s, lru_h, rope)
        x_lat = x_lat + g1 * y_lat
        h2 = modulate(rms_norm(x_lat, norm2[l]), sh2, sc2)
        x_lat = x_lat + g2 * conv_ffn(h2, lp)
    return rms_norm(x_lat, final_norm)
```

```python
import functools
import math

import jax
import jax.numpy as jnp
from jax import lax
from jax.experimental import pallas as pl
from jax.experimental.pallas import tpu as pltpu

F32 = jnp.float32
BF16 = jnp.bfloat16

EPS = 1e-6
N_MOD = 6
RET_QK_DIM = 128
RET_V_DIM = 256
RET_CHUNK = 128
GRID_W = 64
ROPE_BASE = 10000.0
LRU_C = 8.0
ADA_ROWS = 8

V7X_VMEM_BYTES = 64 * 1024 * 1024
VMEM_LIMIT_BYTES = V7X_VMEM_BYTES - 8 * 1024 * 1024


def _params(n_axes):
    return pltpu.CompilerParams(
        dimension_semantics=("arbitrary",) * n_axes,
        vmem_limit_bytes=VMEM_LIMIT_BYTES,
    )


def _sigmoid(x):
    return 1.0 / (1.0 + jnp.exp(-x))


def _silu(x):
    return x * _sigmoid(x)


def _gelu_tanh(x):
    return 0.5 * x * (1.0 + jnp.tanh(math.sqrt(2.0 / math.pi) * (x + 0.044715 * (x * x * x))))


def _softplus(x):
    return jnp.maximum(x, 0.0) + jnp.log1p(jnp.exp(-jnp.abs(x)))


def _log_sigmoid(x):
    return -_softplus(-x)


def _shift_rows(x, k, t_idx):
    n = x.shape[0]
    rolled = pltpu.roll(x, k % n, 0)
    if k > 0:
        return jnp.where(t_idx >= k, rolled, 0.0)
    return jnp.where(t_idx < n + k, rolled, 0.0)


def _ada_kernel(c_ref, w_ref, b_ref, o_ref):
    s = _silu(c_ref[...]).astype(BF16)
    o_ref[...] = jnp.dot(s, w_ref[...].astype(BF16), preferred_element_type=F32) + b_ref[...]


def _ada(c_rows, w, b, *, tn):
    d, n = w.shape
    return pl.pallas_call(
        _ada_kernel,
        grid=(n // tn,),
        in_specs=[
            pl.BlockSpec((ADA_ROWS, d), lambda j: (0, 0)),
            pl.BlockSpec((d, tn), lambda j: (0, j)),
            pl.BlockSpec((1, tn), lambda j: (0, j)),
        ],
        out_specs=pl.BlockSpec((ADA_ROWS, tn), lambda j: (0, j)),
        out_shape=jax.ShapeDtypeStruct((ADA_ROWS, n), F32),
        compiler_params=_params(1),
        name="ada",
    )(c_rows, w, b.reshape(1, n))


def _norm_mod_kernel(x_ref, g_ref, sh_ref, sc_ref, o_ref):
    x = x_ref[...]
    y = x * lax.rsqrt(jnp.mean(x * x, axis=-1, keepdims=True) + EPS)
    o_ref[...] = ((y * g_ref[...]) * (1.0 + sc_ref[...]) + sh_ref[...]).astype(o_ref.dtype)


def _norm_mod(x, gain, shift, scale, *, tm):
    b, t, d = x.shape
    per_batch = shift.shape[0] == b
    mod_idx = (lambda bi, i: (bi, 0, 0)) if per_batch else (lambda bi, i: (0, 0, 0))
    nt = t // tm
    return pl.pallas_call(
        _norm_mod_kernel,
        grid=(b, nt),
        in_specs=[
            pl.BlockSpec((None, tm, d), lambda bi, i: (bi, i, 0)),
            pl.BlockSpec((1, d), lambda bi, i: (0, 0)),
            pl.BlockSpec((None, 1, d), mod_idx),
            pl.BlockSpec((None, 1, d), mod_idx),
        ],
        out_specs=pl.BlockSpec((tm, d), lambda bi, i: (bi * nt + i, 0)),
        out_shape=jax.ShapeDtypeStruct((b * t, d), BF16),
        compiler_params=_params(2),
        name="norm_mod",
    )(x, gain.reshape(1, d), shift, scale)


def _rms_kernel(x_ref, g_ref, o_ref):
    x = x_ref[...]
    y = x * lax.rsqrt(jnp.mean(x * x, axis=-1, keepdims=True) + EPS)
    o_ref[...] = y * g_ref[...]


def _rms(x, gain, *, tm):
    m, d = x.shape
    return pl.pallas_call(
        _rms_kernel,
        grid=(m // tm,),
        in_specs=[pl.BlockSpec((tm, d), lambda i: (i, 0)), pl.BlockSpec((1, d), lambda i: (0, 0))],
        out_specs=pl.BlockSpec((tm, d), lambda i: (i, 0)),
        out_shape=jax.ShapeDtypeStruct((m, d), F32),
        compiler_params=_params(1),
        name="final_rms",
    )(x, gain.reshape(1, d))


def _mm_kernel(x_ref, w_ref, *rest, epilogue, n_extra, nk):
    extra = rest[:n_extra]
    o_ref = rest[n_extra]
    if nk == 1:
        acc = jnp.dot(x_ref[...], w_ref[...], preferred_element_type=F32)
        o_ref[...] = epilogue(acc, *extra).astype(o_ref.dtype)
        return
    acc_ref = rest[n_extra + 1]
    k = pl.program_id(2)

    @pl.when(k == 0)
    def _():
        acc_ref[...] = jnp.zeros_like(acc_ref)

    acc_ref[...] += jnp.dot(x_ref[...], w_ref[...], preferred_element_type=F32)

    @pl.when(k == nk - 1)
    def _():
        o_ref[...] = epilogue(acc_ref[...], *extra).astype(o_ref.dtype)


def _matmul(x, w, *, bm, bn, bk=None, n_cols=None, out_dtype, epilogue=None, extra=(), extra_specs=(), name):
    m, kdim = x.shape
    n = w.shape[1] if n_cols is None else n_cols
    bk = kdim if bk is None else bk
    nk = kdim // bk
    assert m % bm == 0 and n % bn == 0 and kdim % bk == 0
    if epilogue is None:
        epilogue = lambda acc: acc
    kern = functools.partial(_mm_kernel, epilogue=epilogue, n_extra=len(extra), nk=nk)
    return pl.pallas_call(
        kern,
        grid=(m // bm, n // bn, nk),
        in_specs=[
            pl.BlockSpec((bm, bk), lambda i, j, k: (i, k)),
            pl.BlockSpec((bk, bn), lambda i, j, k: (k, j)),
            *extra_specs,
        ],
        out_specs=pl.BlockSpec((bm, bn), lambda i, j, k: (i, j)),
        out_shape=jax.ShapeDtypeStruct((m, n), out_dtype),
        scratch_shapes=[pltpu.VMEM((bm, bn), F32)] if nk > 1 else [],
        compiler_params=_params(3),
        name=name,
    )(x, w, *extra)


def _ep_gate(acc, m_ref):
    return _sigmoid(m_ref[...].astype(F32)) * acc


def _ep_gate_add(acc, m_ref, y_ref):
    return _sigmoid(m_ref[...].astype(F32)) * acc + y_ref[...].astype(F32)


def _ep_resid(acc, x_ref, g_ref):
    return x_ref[...] + g_ref[...] * acc


def _ret_kernel(q_ref, k_ref, v_ref, g_ref, kc_ref, vc_ref, cos_ref, sin_ref, dl_ref, o_ref,
                qs_ref, ks_ref, r_ref, *, seq, ctx_len):
    h = pl.program_id(1)
    n_heads = pl.num_programs(1)
    c = RET_CHUNK
    dk = RET_QK_DIM
    n_chunks = seq // c
    scale = dk ** -0.5
    tn_dims = (((0,), (0,)), ((), ()))
    nt_dims = (((1,), (1,)), ((), ()))

    lgf = _log_sigmoid(dl_ref[pl.ds(h, 1), :])
    lgb = _log_sigmoid(dl_ref[pl.ds(n_heads + h, 1), :])
    lgf_k = lgf[:, :dk]
    lgb_k = lgb[:, :dk]

    ri = lax.broadcasted_iota(jnp.int32, (c, c), 0).astype(F32)
    ci = lax.broadcasted_iota(jnp.int32, (c, c), 1).astype(F32)
    rel = ri - ci
    dmask = jnp.where(rel > 0, jnp.exp(jnp.maximum(rel, 0.0) * lgf_k),
                      jnp.where(rel < 0, jnp.exp(jnp.maximum(-rel, 0.0) * lgb_k), 2.0))
    qdf = jnp.exp((ri + 1.0) * lgf_k)
    qdb = jnp.exp((c - ri) * lgb_k)
    kdf = jnp.exp((c - 1.0 - ri) * lgf_k)
    kdb = jnp.exp(ri * lgb_k)
    cdf = jnp.exp(c * lgf)
    cdb = jnp.exp(c * lgb)

    li = lax.broadcasted_iota(jnp.int32, (ctx_len, dk), 0).astype(F32)
    kc = kc_ref[...].astype(F32) * scale
    vc = vc_ref[...]
    s0f = lax.dot_general((kc * jnp.exp((ctx_len - 1.0 - li) * lgf_k)).astype(BF16), vc, tn_dims,
                          preferred_element_type=F32)
    s0b = lax.dot_general((kc * jnp.exp(li * lgb_k)).astype(BF16), vc, tn_dims,
                          preferred_element_type=F32)

    cos = cos_ref[...]
    sin = sin_ref[...]
    q = q_ref[...].astype(F32)
    qs_ref[...] = (q * cos + pltpu.roll(q, dk // 2, 1) * sin).astype(BF16)
    k = k_ref[...].astype(F32)
    ks_ref[...] = ((k * cos + pltpu.roll(k, dk // 2, 1) * sin) * scale).astype(BF16)

    def rows(j):
        return pl.ds(pl.multiple_of(j * c, c), c)

    def bwd(i, r):
        j = n_chunks - 1 - i
        r_ref[j] = r.astype(BF16)
        kj = (ks_ref[rows(j), :].astype(F32) * kdb).astype(BF16)
        return cdb * r + lax.dot_general(kj, v_ref[rows(j), :], tn_dims, preferred_element_type=F32)

    lax.fori_loop(0, n_chunks, bwd, s0b)

    def fwd(j, s):
        qj = qs_ref[rows(j), :]
        kj = ks_ref[rows(j), :]
        vj = v_ref[rows(j), :]
        qk = lax.dot_general(qj, kj, nt_dims, preferred_element_type=F32)
        o = jnp.dot((qk * dmask).astype(BF16), vj, preferred_element_type=F32)
        qf = qj.astype(F32)
        qd = jnp.concatenate([(qf * qdf).astype(BF16), (qf * qdb).astype(BF16)], axis=1)
        st = jnp.concatenate([s.astype(BF16), r_ref[j]], axis=0)
        o = o + jnp.dot(qd, st, preferred_element_type=F32)
        mu = jnp.mean(o, axis=-1, keepdims=True)
        d = o - mu
        var = jnp.mean(d * d, axis=-1, keepdims=True)
        g = g_ref[rows(j), :].astype(F32)
        o_ref[rows(j), :] = (_silu(g) * (d * lax.rsqrt(var + EPS))).astype(o_ref.dtype)
        kd = (kj.astype(F32) * kdf).astype(BF16)
        return cdf * s + lax.dot_general(kd, vj, tn_dims, preferred_element_type=F32)

    lax.fori_loop(0, n_chunks, fwd, s0f)


def _retention(p, pc, cos2, sin2, decay_logit, *, batch, seq, ctx_len, n_heads, col):
    dk, dv = RET_QK_DIM, RET_V_DIM
    dl = jnp.broadcast_to(decay_logit.astype(F32).reshape(2 * n_heads, 1), (2 * n_heads, dv))
    kern = functools.partial(_ret_kernel, seq=seq, ctx_len=ctx_len)
    kb, vb, qb, gb = col["k"] // dk, col["v"] // dv, col["q"] // dk, col["g_ret"] // dv
    return pl.pallas_call(
        kern,
        grid=(batch, n_heads),
        in_specs=[
            pl.BlockSpec((seq, dk), lambda b, h: (b, qb + h)),
            pl.BlockSpec((seq, dk), lambda b, h: (b, kb + h)),
            pl.BlockSpec((seq, dv), lambda b, h: (b, vb + h)),
            pl.BlockSpec((seq, dv), lambda b, h: (b, gb + h)),
            pl.BlockSpec((ctx_len, dk), lambda b, h: (b, kb + h)),
            pl.BlockSpec((ctx_len, dv), lambda b, h: (b, vb + h)),
            pl.BlockSpec((seq, dk), lambda b, h: (0, 0)),
            pl.BlockSpec((seq, dk), lambda b, h: (0, 0)),
            pl.BlockSpec((2 * n_heads, dv), lambda b, h: (0, 0)),
        ],
        out_specs=pl.BlockSpec((seq, dv), lambda b, h: (b, h)),
        out_shape=jax.ShapeDtypeStruct((batch * seq, n_heads * dv), BF16),
        scratch_shapes=[
            pltpu.VMEM((seq, dk), BF16),
            pltpu.VMEM((seq, dk), BF16),
            pltpu.VMEM((seq // RET_CHUNK, dk, dv), BF16),
        ],
        compiler_params=_params(2),
        name="retention",
    )(p, p, p, p, pc, pc, cos2, sin2, dl)


def _rope_tables(seq):
    rows = seq // GRID_W
    row_ids = jnp.repeat(jnp.arange(rows, dtype=F32), GRID_W)
    col_ids = jnp.tile(jnp.arange(GRID_W, dtype=F32), rows)
    n_freq = RET_QK_DIM // 4
    inv_freq = ROPE_BASE ** (-jnp.arange(n_freq, dtype=F32) / n_freq)
    ang = jnp.concatenate([row_ids[:, None] * inv_freq, col_ids[:, None] * inv_freq], axis=-1)
    cos, sin = jnp.cos(ang), jnp.sin(ang)
    return jnp.concatenate([cos, cos], axis=-1), jnp.concatenate([-sin, sin], axis=-1)


def _lru_kernel(*refs, seq, final_only):
    if final_only:
        x_ref, cw_ref, cb_ref, wa_ref, wx_ref, ba_ref, bx_ref, lam_ref, h0_ref, o_ref, a_scr, b_scr, y_scr = refs
        g_ref = None
    else:
        (x_ref, cw_ref, cb_ref, wa_ref, wx_ref, ba_ref, bx_ref, lam_ref, h0_ref, g_ref, o_ref,
         a_scr, b_scr, y_scr) = refs
    x = x_ref[...].astype(F32)
    t_idx = lax.broadcasted_iota(jnp.int32, x.shape, 0)
    cw = cw_ref[...]
    xc = (cw[0:1] * _shift_rows(x, 2, t_idx) + cw[1:2] * _shift_rows(x, 1, t_idx) + cw[2:3] * x
          + cw[3:4] * _shift_rows(x, -1, t_idx) + cb_ref[...])
    xcb = xc.astype(BF16)
    for z in range(2):
        gate_r = jnp.dot(xcb, wa_ref[z], preferred_element_type=F32) + ba_ref[z:z + 1, :]
        gate_i = jnp.dot(xcb, wx_ref[z], preferred_element_type=F32) + bx_ref[z:z + 1, :]
        log_a = (-LRU_C * _softplus(-lam_ref[z:z + 1, :])) * _sigmoid(gate_r)
        a_scr[z] = jnp.exp(log_a)
        th = jnp.tanh(log_a)
        b_scr[z] = jnp.sqrt(-2.0 * th / (1.0 - th)) * _sigmoid(gate_i) * xc

    def step(s, carry):
        hf, hb = carry
        tb = seq - 1 - s
        hf = a_scr[0, pl.ds(s, 1), :] * hf + b_scr[0, pl.ds(s, 1), :]
        y_scr[0, pl.ds(s, 1), :] = hf
        hb = a_scr[1, pl.ds(tb, 1), :] * hb + b_scr[1, pl.ds(tb, 1), :]
        y_scr[1, pl.ds(tb, 1), :] = hb
        return hf, hb

    h0 = h0_ref[...]
    hf, hb = lax.fori_loop(0, seq, step, (h0[0:1], h0[1:2]), unroll=8)
    if final_only:
        o_ref[...] = jnp.concatenate([hf, hb], axis=0)
    else:
        g = g_ref[...].astype(F32)
        o_ref[...] = (_gelu_tanh(g) * (y_scr[0] + y_scr[1])).astype(o_ref.dtype)


def _lru(p, conv_w, conv_b, wa, wx, ba, bx, lam, h0, *, batch, seq, col, final_only):
    nb, blk = wa.shape[1], wa.shape[2]
    w = nb * blk
    xb = col["x_lru"] // blk
    in_specs = [
        pl.BlockSpec((seq, blk), lambda b, n: (b, xb + n)),
        pl.BlockSpec((conv_w.shape[0], blk), lambda b, n: (0, n)),
        pl.BlockSpec((1, blk), lambda b, n: (0, n)),
        pl.BlockSpec((2, None, blk, blk), lambda b, n: (0, n, 0, 0)),
        pl.BlockSpec((2, None, blk, blk), lambda b, n: (0, n, 0, 0)),
        pl.BlockSpec((2, blk), lambda b, n: (0, n)),
        pl.BlockSpec((2, blk), lambda b, n: (0, n)),
        pl.BlockSpec((2, blk), lambda b, n: (0, n)),
        pl.BlockSpec((None, 2, blk), lambda b, n: (b, 0, n)),
    ]
    args = [p, conv_w, conv_b.reshape(1, w), wa, wx, ba, bx, lam, h0]
    if final_only:
        out_spec = pl.BlockSpec((None, 2, blk), lambda b, n: (b, 0, n))
        out_shape = jax.ShapeDtypeStruct((batch, 2, w), F32)
    else:
        gb = col["g_lru"] // blk
        in_specs.append(pl.BlockSpec((seq, blk), lambda b, n: (b, gb + n)))
        args.append(p)
        out_spec = pl.BlockSpec((seq, blk), lambda b, n: (b, n))
        out_shape = jax.ShapeDtypeStruct((batch * seq, w), BF16)
    return pl.pallas_call(
        functools.partial(_lru_kernel, seq=seq, final_only=final_only),
        grid=(batch, nb),
        in_specs=in_specs,
        out_specs=out_spec,
        out_shape=out_shape,
        scratch_shapes=[pltpu.VMEM((2, seq, blk), F32)] * 3,
        compiler_params=_params(2),
        name="lru_ctx" if final_only else "lru",
    )(*args)


def _ffn_up_kernel(x_ref, wg_ref, wv_ref, cwg_ref, cwv_ref, cbg_ref, cbv_ref, o_ref):
    x = x_ref[...]
    t_idx = lax.broadcasted_iota(jnp.int32, o_ref.shape, 0)

    def conv(u, cw, cb):
        return cw[0:1] * _shift_rows(u, 1, t_idx) + cw[1:2] * u + cw[2:3] * _shift_rows(u, -1, t_idx) + cb

    g = conv(jnp.dot(x, wg_ref[...], preferred_element_type=F32), cwg_ref[...], cbg_ref[...])
    v = conv(jnp.dot(x, wv_ref[...], preferred_element_type=F32), cwv_ref[...], cbv_ref[...])
    o_ref[...] = (_silu(g) * v).astype(o_ref.dtype)


def _ffn_up(h, w_up, conv_w, conv_b, *, batch, seq, bn):
    d = h.shape[1]
    f = w_up.shape[1] // 2
    nj = f // bn
    cb = conv_b.reshape(1, 2 * f)
    taps = conv_w.shape[0]
    return pl.pallas_call(
        _ffn_up_kernel,
        grid=(batch, nj),
        in_specs=[
            pl.BlockSpec((seq, d), lambda b, j: (b, 0)),
            pl.BlockSpec((d, bn), lambda b, j: (0, j)),
            pl.BlockSpec((d, bn), lambda b, j: (0, nj + j)),
            pl.BlockSpec((taps, bn), lambda b, j: (0, j)),
            pl.BlockSpec((taps, bn), lambda b, j: (0, nj + j)),
            pl.BlockSpec((1, bn), lambda b, j: (0, j)),
            pl.BlockSpec((1, bn), lambda b, j: (0, nj + j)),
        ],
        out_specs=pl.BlockSpec((seq, bn), lambda b, j: (b, j)),
        out_shape=jax.ShapeDtypeStruct((batch * seq, f), BF16),
        compiler_params=_params(2),
        name="ffn_up",
    )(h, w_up, w_up, conv_w, conv_w, cb, cb)


def _tile(n, pref):
    return pref if n % pref == 0 else n


def kernel(x, c, ctx, c_ctx, w_ada, b_ada, norm1, norm2, w_in, ret_decay_logit, lru_conv_w, lru_conv_b,
           lru_wa, lru_ba, lru_wx, lru_bx, lru_lambda, w_ret_o, w_lru_o, w_out, w_up, ffn_conv_w,
           ffn_conv_b, w_down, final_norm):
    batch, seq, d = x.shape
    ctx_len = ctx.shape[1]
    assert w_in.shape[0] == 1, "single-layer trunk"
    assert batch + 1 <= ADA_ROWS
    n_heads = ret_decay_logit.shape[-1]
    qk_w, v_w, lru_w = n_heads * RET_QK_DIM, n_heads * RET_V_DIM, lru_lambda.shape[-1]
    sizes = (("k", qk_w), ("v", v_w), ("x_lru", lru_w), ("q", qk_w), ("g_ret", v_w), ("g_lru", lru_w),
             ("m_ret", d), ("m_lru", d))
    col, off = {}, 0
    for name, size in sizes:
        col[name] = off
        off += size
    state_cols = col["q"]
    m = batch * seq
    f = w_down.shape[1]

    bf = lambda a: a.astype(BF16)
    w_in_b, w_ret_b, w_lru_b, w_out_b = bf(w_in[0]), bf(w_ret_o[0]), bf(w_lru_o[0]), bf(w_out[0])
    w_up_b, w_down_b = bf(w_up[0]), bf(w_down[0])
    wa_b, wx_b = bf(lru_wa[0]), bf(lru_wx[0])

    c_rows = jnp.zeros((ADA_ROWS, d), F32).at[:batch].set(c).at[batch].set(c_ctx)
    mod = _ada(c_rows, w_ada[0], b_ada[0], tn=_tile(N_MOD * d, 512))
    sh1, sc1, g1, sh2, sc2, g2 = [mod[:batch, i * d:(i + 1) * d].reshape(batch, 1, d) for i in range(N_MOD)]
    csh, csc = [mod[batch, i * d:(i + 1) * d].reshape(1, 1, d) for i in range(2)]

    bm = _tile(seq, 1024)
    bn = 1024

    h_ctx = _norm_mod(ctx, norm1[0], csh, csc, tm=_tile(ctx_len, 256))
    pc = _matmul(h_ctx, w_in_b, bm=_tile(batch * ctx_len, 1024), bn=_tile(state_cols, bn), n_cols=state_cols,
                 out_dtype=BF16, name="w_in_ctx")
    lru_args = (lru_conv_w[0], lru_conv_b[0], wa_b, wx_b, lru_ba[0], lru_bx[0], lru_lambda[0])
    h0 = _lru(pc, *lru_args, jnp.zeros((batch, 2, lru_w), F32), batch=batch, seq=ctx_len, col=col,
              final_only=True)

    h_lat = _norm_mod(x, norm1[0], sh1, sc1, tm=_tile(seq, 256))
    p = _matmul(h_lat, w_in_b, bm=bm, bn=_tile(off, bn), out_dtype=BF16, name="w_in")
    cos2, sin2 = _rope_tables(seq)
    a_ret = _retention(p, pc, cos2, sin2, ret_decay_logit[0], batch=batch, seq=seq, ctx_len=ctx_len,
                       n_heads=n_heads, col=col)
    a_lru = _lru(p, *lru_args, h0, batch=batch, seq=seq, col=col, final_only=False)

    bnd = _tile(d, bn)
    gate_spec = lambda name: pl.BlockSpec((bm, bnd), lambda i, j, k: (i, col[name] // bnd + j))
    tile_spec = pl.BlockSpec((bm, bnd), lambda i, j, k: (i, j))
    y1 = _matmul(a_ret, w_ret_b, bm=bm, bn=bnd, out_dtype=BF16, epilogue=_ep_gate, extra=(p,),
                 extra_specs=(gate_spec("m_ret"),), name="w_ret_o")
    y = _matmul(a_lru, w_lru_b, bm=bm, bn=bnd, out_dtype=BF16, epilogue=_ep_gate_add, extra=(p, y1),
                extra_specs=(gate_spec("m_lru"), tile_spec), name="w_lru_o")
    steps_per_batch = seq // bm
    vec_spec = pl.BlockSpec((None, 1, bnd), lambda i, j, k: (i // steps_per_batch, 0, j))
    x2d = x.reshape(m, d)
    x_lat = _matmul(y, w_out_b, bm=bm, bn=bnd, out_dtype=F32, epilogue=_ep_resid, extra=(x2d, g1),
                    extra_specs=(tile_spec, vec_spec), name="w_out")

    h2 = _norm_mod(x_lat.reshape(batch, seq, d), norm2[0], sh2, sc2, tm=_tile(seq, 256))
    act = _ffn_up(h2, w_up_b, ffn_conv_w[0], ffn_conv_b[0], batch=batch, seq=seq, bn=_tile(f, 256))
    bk = f // 2 if (f // 2) % 128 == 0 else f
    bm_d = _tile(seq, 512)
    spb_d = seq // bm_d
    x_lat = _matmul(act, w_down_b, bm=bm_d, bn=bnd, bk=bk, out_dtype=F32, epilogue=_ep_resid,
                    extra=(x_lat, g2),
                    extra_specs=(pl.BlockSpec((bm_d, bnd), lambda i, j, k: (i, j)),
                                 pl.BlockSpec((None, 1, bnd), lambda i, j, k: (i // spb_d, 0, j))),
                    name="w_down")
    out = _rms(x_lat, final_norm, tm=_tile(m, 256))
    return out.reshape(batch, seq, d)
```

```python
import functools
import math

import jax
import jax.numpy as jnp
from jax import lax
from jax.experimental import pallas as pl
from jax.experimental.pallas import tpu as pltpu

F32 = jnp.float32
BF16 = jnp.bfloat16

EPS = 1e-6
N_MOD = 6
RET_QK_DIM = 128
RET_V_DIM = 256
RET_CHUNK = 128
GRID_W = 64
ROPE_BASE = 10000.0
LRU_C = 8.0
SUBLANES = 8
LANES = 128
ADA_ROWS = SUBLANES
LRU_ROWS = 128

V7X_VMEM_BYTES = 64 * 1024 * 1024
VMEM_LIMIT_BYTES = V7X_VMEM_BYTES - 8 * 1024 * 1024


def _params(n_axes):
    return pltpu.CompilerParams(
        dimension_semantics=("arbitrary",) * n_axes,
        vmem_limit_bytes=VMEM_LIMIT_BYTES,
    )


def _sigmoid(x):
    return 0.5 * jnp.tanh(0.5 * x) + 0.5


def _silu(x):
    return x * _sigmoid(x)


def _gelu_tanh(x):
    k = math.sqrt(2.0 / math.pi)
    half = 0.5 * x
    return half * jnp.tanh(x * ((k * 0.044715) * (x * x) + k)) + half


def _softplus(x):
    return jnp.maximum(x, 0.0) + jnp.log1p(jnp.exp(-jnp.abs(x)))


def _log_sigmoid(x):
    return -_softplus(-x)


def _shift_rows(x, k, t_idx):
    n = x.shape[0]
    rolled = pltpu.roll(x, k % n, 0)
    if k > 0:
        return jnp.where(t_idx >= k, rolled, 0.0)
    return jnp.where(t_idx < n + k, rolled, 0.0)


def _ada_kernel(c_ref, w_ref, b_ref, o_ref):
    s = _silu(c_ref[...]).astype(BF16)
    o_ref[...] = jnp.dot(s, w_ref[...].astype(BF16), preferred_element_type=F32) + b_ref[...]


def _ada(c_rows, w, b, *, tn):
    d, n = w.shape
    return pl.pallas_call(
        _ada_kernel,
        grid=(n // tn,),
        in_specs=[
            pl.BlockSpec((ADA_ROWS, d), lambda j: (0, 0)),
            pl.BlockSpec((d, tn), lambda j: (0, j)),
            pl.BlockSpec((1, tn), lambda j: (0, j)),
        ],
        out_specs=pl.BlockSpec((ADA_ROWS, tn), lambda j: (0, j)),
        out_shape=jax.ShapeDtypeStruct((ADA_ROWS, n), F32),
        compiler_params=_params(1),
        name="ada",
    )(c_rows, w, b.reshape(1, n))


def _norm_mod_kernel(x_ref, g_ref, sh_ref, sc_ref, o_ref):
    x = x_ref[...]
    y = x * lax.rsqrt(jnp.mean(x * x, axis=-1, keepdims=True) + EPS)
    o_ref[...] = ((y * g_ref[...]) * (1.0 + sc_ref[...]) + sh_ref[...]).astype(o_ref.dtype)


def _norm_mod(x, gain, shift, scale, *, tm):
    b, t, d = x.shape
    per_batch = shift.shape[0] == b
    mod_idx = (lambda bi, i: (bi, 0, 0)) if per_batch else (lambda bi, i: (0, 0, 0))
    nt = t // tm
    return pl.pallas_call(
        _norm_mod_kernel,
        grid=(b, nt),
        in_specs=[
            pl.BlockSpec((None, tm, d), lambda bi, i: (bi, i, 0)),
            pl.BlockSpec((1, d), lambda bi, i: (0, 0)),
            pl.BlockSpec((None, 1, d), mod_idx),
            pl.BlockSpec((None, 1, d), mod_idx),
        ],
        out_specs=pl.BlockSpec((tm, d), lambda bi, i: (bi * nt + i, 0)),
        out_shape=jax.ShapeDtypeStruct((b * t, d), BF16),
        compiler_params=_params(2),
        name="norm_mod",
    )(x, gain.reshape(1, d), shift, scale)


def _rms_kernel(x_ref, g_ref, o_ref):
    x = x_ref[...]
    y = x * lax.rsqrt(jnp.mean(x * x, axis=-1, keepdims=True) + EPS)
    o_ref[...] = y * g_ref[...]


def _rms(x, gain, *, tm):
    m, d = x.shape
    return pl.pallas_call(
        _rms_kernel,
        grid=(m // tm,),
        in_specs=[pl.BlockSpec((tm, d), lambda i: (i, 0)), pl.BlockSpec((1, d), lambda i: (0, 0))],
        out_specs=pl.BlockSpec((tm, d), lambda i: (i, 0)),
        out_shape=jax.ShapeDtypeStruct((m, d), F32),
        compiler_params=_params(1),
        name="final_rms",
    )(x, gain.reshape(1, d))


def _mm_kernel(x_ref, w_ref, *rest, epilogue, n_extra, nk):
    extra = rest[:n_extra]
    o_ref = rest[n_extra]
    if nk == 1:
        acc = jnp.dot(x_ref[...], w_ref[...], preferred_element_type=F32)
        o_ref[...] = epilogue(acc, *extra).astype(o_ref.dtype)
        return
    acc_ref = rest[n_extra + 1]
    k = pl.program_id(2)

    @pl.when(k == 0)
    def _():
        acc_ref[...] = jnp.zeros_like(acc_ref)

    acc_ref[...] += jnp.dot(x_ref[...], w_ref[...], preferred_element_type=F32)

    @pl.when(k == nk - 1)
    def _():
        o_ref[...] = epilogue(acc_ref[...], *extra).astype(o_ref.dtype)


def _matmul(x, w, *, bm, bn, bk=None, n_cols=None, out_dtype, epilogue=None, extra=(), extra_specs=(), name):
    m, kdim = x.shape
    n = w.shape[1] if n_cols is None else n_cols
    bk = kdim if bk is None else bk
    nk = kdim // bk
    assert m % bm == 0 and n % bn == 0 and kdim % bk == 0
    if epilogue is None:
        epilogue = lambda acc: acc
    kern = functools.partial(_mm_kernel, epilogue=epilogue, n_extra=len(extra), nk=nk)
    return pl.pallas_call(
        kern,
        grid=(m // bm, n // bn, nk),
        in_specs=[
            pl.BlockSpec((bm, bk), lambda i, j, k: (i, k)),
            pl.BlockSpec((bk, bn), lambda i, j, k: (k, j)),
            *extra_specs,
        ],
        out_specs=pl.BlockSpec((bm, bn), lambda i, j, k: (i, j)),
        out_shape=jax.ShapeDtypeStruct((m, n), out_dtype),
        scratch_shapes=[pltpu.VMEM((bm, bn), F32)] if nk > 1 else [],
        compiler_params=_params(3),
        name=name,
    )(x, w, *extra)


def _ep_gate(acc, m_ref):
    return _sigmoid(m_ref[...].astype(F32)) * acc


def _ep_gate_add(acc, m_ref, y_ref):
    return _sigmoid(m_ref[...].astype(F32)) * acc + y_ref[...].astype(F32)


def _ep_resid(acc, x_ref, g_ref):
    return x_ref[...] + g_ref[...] * acc


def _ret_kernel(q_ref, k_ref, v_ref, g_ref, kc_ref, vc_ref, cos_ref, sin_ref, dl_ref, o_ref,
                qs_ref, ks_ref, r_ref, *, seq, ctx_len):
    h = pl.program_id(1)
    n_heads = pl.num_programs(1)
    c = RET_CHUNK
    dk = RET_QK_DIM
    n_chunks = seq // c
    scale = dk ** -0.5
    tn_dims = (((0,), (0,)), ((), ()))
    nt_dims = (((1,), (1,)), ((), ()))

    lgf = _log_sigmoid(dl_ref[pl.ds(h, 1), :])
    lgb = _log_sigmoid(dl_ref[pl.ds(n_heads + h, 1), :])
    lgf_k = lgf[:, :dk]
    lgb_k = lgb[:, :dk]

    ri = lax.broadcasted_iota(jnp.int32, (c, c), 0).astype(F32)
    ci = lax.broadcasted_iota(jnp.int32, (c, c), 1).astype(F32)
    rel = ri - ci
    dmask = jnp.where(rel > 0, jnp.exp(jnp.maximum(rel, 0.0) * lgf_k),
                      jnp.where(rel < 0, jnp.exp(jnp.maximum(-rel, 0.0) * lgb_k), 2.0))
    qdf = jnp.exp((ri + 1.0) * lgf_k)
    qdb = jnp.exp((c - ri) * lgb_k)
    kdf = jnp.exp((c - 1.0 - ri) * lgf_k)
    kdb = jnp.exp(ri * lgb_k)
    cdf = jnp.exp(c * lgf)
    cdb = jnp.exp(c * lgb)

    li = lax.broadcasted_iota(jnp.int32, (ctx_len, dk), 0).astype(F32)
    kc = kc_ref[...].astype(F32) * scale
    vc = vc_ref[...]
    s0f = lax.dot_general((kc * jnp.exp((ctx_len - 1.0 - li) * lgf_k)).astype(BF16), vc, tn_dims,
                          preferred_element_type=F32)
    s0b = lax.dot_general((kc * jnp.exp(li * lgb_k)).astype(BF16), vc, tn_dims,
                          preferred_element_type=F32)

    cos = cos_ref[...]
    sin = sin_ref[...]
    q = q_ref[...].astype(F32)
    qs_ref[...] = (q * cos + pltpu.roll(q, dk // 2, 1) * sin).astype(BF16)
    k = k_ref[...].astype(F32)
    ks_ref[...] = ((k * cos + pltpu.roll(k, dk // 2, 1) * sin) * scale).astype(BF16)

    def rows(j):
        return pl.ds(pl.multiple_of(j * c, c), c)

    def bwd(i, r):
        j = n_chunks - 1 - i
        r_ref[j] = r.astype(BF16)
        kj = (ks_ref[rows(j), :].astype(F32) * kdb).astype(BF16)
        return cdb * r + lax.dot_general(kj, v_ref[rows(j), :], tn_dims, preferred_element_type=F32)

    lax.fori_loop(0, n_chunks, bwd, s0b, unroll=4)

    def fwd(j, s):
        qj = qs_ref[rows(j), :]
        kj = ks_ref[rows(j), :]
        vj = v_ref[rows(j), :]
        qk = lax.dot_general(qj, kj, nt_dims, preferred_element_type=F32)
        o = jnp.dot((qk * dmask).astype(BF16), vj, preferred_element_type=F32)
        qf = qj.astype(F32)
        qd = jnp.concatenate([(qf * qdf).astype(BF16), (qf * qdb).astype(BF16)], axis=1)
        st = jnp.concatenate([s.astype(BF16), r_ref[j]], axis=0)
        o = o + jnp.dot(qd, st, preferred_element_type=F32)
        mu = jnp.mean(o, axis=-1, keepdims=True)
        d = o - mu
        var = jnp.mean(d * d, axis=-1, keepdims=True)
        g = g_ref[rows(j), :].astype(F32)
        o_ref[rows(j), :] = (_silu(g) * (d * lax.rsqrt(var + EPS))).astype(o_ref.dtype)
        kd = (kj.astype(F32) * kdf).astype(BF16)
        return cdf * s + lax.dot_general(kd, vj, tn_dims, preferred_element_type=F32)

    lax.fori_loop(0, n_chunks, fwd, s0f, unroll=4)


def _retention(p, pc, cos2, sin2, decay_logit, *, batch, seq, ctx_len, n_heads, col):
    dk, dv = RET_QK_DIM, RET_V_DIM
    dl = jnp.broadcast_to(decay_logit.astype(F32).reshape(2 * n_heads, 1), (2 * n_heads, dv))
    kern = functools.partial(_ret_kernel, seq=seq, ctx_len=ctx_len)
    kb, vb, qb, gb = col["k"] // dk, col["v"] // dv, col["q"] // dk, col["g_ret"] // dv
    return pl.pallas_call(
        kern,
        grid=(batch, n_heads),
        in_specs=[
            pl.BlockSpec((seq, dk), lambda b, h: (b, qb + h)),
            pl.BlockSpec((seq, dk), lambda b, h: (b, kb + h)),
            pl.BlockSpec((seq, dv), lambda b, h: (b, vb + h)),
            pl.BlockSpec((seq, dv), lambda b, h: (b, gb + h)),
            pl.BlockSpec((ctx_len, dk), lambda b, h: (b, kb + h)),
            pl.BlockSpec((ctx_len, dv), lambda b, h: (b, vb + h)),
            pl.BlockSpec((seq, dk), lambda b, h: (0, 0)),
            pl.BlockSpec((seq, dk), lambda b, h: (0, 0)),
            pl.BlockSpec((2 * n_heads, dv), lambda b, h: (0, 0)),
        ],
        out_specs=pl.BlockSpec((seq, dv), lambda b, h: (b, h)),
        out_shape=jax.ShapeDtypeStruct((batch * seq, n_heads * dv), BF16),
        scratch_shapes=[
            pltpu.VMEM((seq, dk), BF16),
            pltpu.VMEM((seq, dk), BF16),
            pltpu.VMEM((seq // RET_CHUNK, dk, dv), BF16),
        ],
        compiler_params=_params(2),
        name="retention",
    )(p, p, p, p, pc, pc, cos2, sin2, dl)


def _rope_tables(seq):
    rows = seq // GRID_W
    row_ids = jnp.repeat(jnp.arange(rows, dtype=F32), GRID_W)
    col_ids = jnp.tile(jnp.arange(GRID_W, dtype=F32), rows)
    n_freq = RET_QK_DIM // 4
    inv_freq = ROPE_BASE ** (-jnp.arange(n_freq, dtype=F32) / n_freq)
    ang = jnp.concatenate([row_ids[:, None] * inv_freq, col_ids[:, None] * inv_freq], axis=-1)
    cos, sin = jnp.cos(ang), jnp.sin(ang)
    return jnp.concatenate([cos, cos], axis=-1), jnp.concatenate([-sin, sin], axis=-1)


def _lru_kernel(*refs, seq, final_only):
    if final_only:
        (x_ref, cw_ref, cb_ref, wa_ref, wx_ref, ba_ref, bx_ref, lam_ref, h0_ref, o_ref,
         xs_scr, xp_scr, a_scr, b_scr) = refs
        g_ref = y_scr = None
    else:
        (x_ref, cw_ref, cb_ref, wa_ref, wx_ref, ba_ref, bx_ref, lam_ref, h0_ref, g_ref, o_ref,
         xs_scr, xp_scr, a_scr, b_scr, y_scr) = refs
    blk = x_ref.shape[1]
    n_slab = blk // LANES
    seg = seq // SUBLANES
    pitch = seg + SUBLANES
    taps = cw_ref.shape[0]
    lead = taps - 2
    lanes = [slice(l * LANES, (l + 1) * LANES) for l in range(n_slab)]
    sub = lax.broadcasted_iota(jnp.int32, (SUBLANES, LANES), 0)

    for s in range(SUBLANES):
        xs = x_ref[s * seg:(s + 1) * seg, :].astype(F32)
        for l in range(n_slab):
            xs_scr[l, s * pitch:s * pitch + seg, :] = xs[:, lanes[l]]

    def tile(j):
        return pl.ds(pl.multiple_of(j * SUBLANES, SUBLANES), SUBLANES)

    def gather(j, carry):
        for l in range(n_slab):
            xp_scr[l, tile(j + lead), :] = xs_scr[l, pl.ds(j, SUBLANES, stride=pitch), :]
        return carry

    lax.fori_loop(0, seg, gather, 0, unroll=8)
    for l in range(n_slab):
        for k in range(lead):
            prev = xp_scr[l, (seg + k) * SUBLANES:(seg + k + 1) * SUBLANES, :]
            xp_scr[l, k * SUBLANES:(k + 1) * SUBLANES, :] = jnp.where(sub == 0, 0.0, pltpu.roll(prev, 1, 0))
        nxt = xp_scr[l, lead * SUBLANES:(lead + 1) * SUBLANES, :]
        xp_scr[l, (seg + lead) * SUBLANES:(seg + lead + 1) * SUBLANES, :] = jnp.where(
            sub == SUBLANES - 1, 0.0, pltpu.roll(nxt, SUBLANES - 1, 0))

    cw = cw_ref[...]
    cb = cb_ref[...]
    rate = [(0.25 * LRU_C) * _softplus(-lam_ref[z:z + 1, :]) for z in range(2)]

    def chunk(c, carry):
        r0 = pl.multiple_of(c * LRU_ROWS, LRU_ROWS)
        parts = []
        for l in range(n_slab):
            acc = cb[:, lanes[l]]
            for k in range(taps):
                acc = acc + cw[k:k + 1, lanes[l]] * xp_scr[l, pl.ds(r0 + k * SUBLANES, LRU_ROWS), :]
            parts.append(acc)
        xc = jnp.concatenate(parts, axis=1)
        xcb = xc.astype(BF16)
        for z in range(2):
            tr = jnp.tanh(jnp.dot(xcb, wa_ref[z], preferred_element_type=F32) + ba_ref[z:z + 1, :])
            ti = jnp.tanh(jnp.dot(xcb, wx_ref[z], preferred_element_type=F32) + bx_ref[z:z + 1, :])
            v = jnp.tanh(rate[z] * tr + rate[z])
            inv = 1.0 / (1.0 + v)
            a = (1.0 - v) * inv
            root = jnp.where(v > 0.0, v * lax.rsqrt(v), 0.0)
            b = (root * inv) * (xc * ti + xc)
            for l in range(n_slab):
                a_scr[z * n_slab + l, pl.ds(r0, LRU_ROWS), :] = a[:, lanes[l]]
                b_scr[z * n_slab + l, pl.ds(r0, LRU_ROWS), :] = b[:, lanes[l]]
        return carry

    lax.fori_loop(0, seq // LRU_ROWS, chunk, 0, unroll=2)

    def pass1(j, carry):
        jb = seg - 1 - j
        out = []
        for l in range(n_slab):
            hf, pf, hb, pb = carry[4 * l:4 * l + 4]
            af = a_scr[l, tile(j), :]
            ab = a_scr[n_slab + l, tile(jb), :]
            out += [af * hf + b_scr[l, tile(j), :], af * pf,
                    ab * hb + b_scr[n_slab + l, tile(jb), :], ab * pb]
        return tuple(out)

    zero = jnp.zeros((SUBLANES, LANES), F32)
    one = jnp.ones((SUBLANES, LANES), F32)
    ends = lax.fori_loop(0, seg, pass1, (zero, one, zero, one) * n_slab, unroll=8)

    h0 = h0_ref[...]
    starts = []
    finals = []
    for l in range(n_slab):
        hf, pf, hb, pb = ends[4 * l:4 * l + 4]
        h0f = jnp.broadcast_to(h0[0:1, lanes[l]], (SUBLANES, LANES))
        h0b = jnp.broadcast_to(h0[1:2, lanes[l]], (SUBLANES, LANES))
        cf, cbk = h0f, h0b
        for _ in range(SUBLANES - 1):
            cf = jnp.where(sub == 0, h0f, pltpu.roll(hf + pf * cf, 1, 0))
            cbk = jnp.where(sub == SUBLANES - 1, h0b, pltpu.roll(hb + pb * cbk, SUBLANES - 1, 0))
        starts += [cf, cbk]
        finals += [(hf + pf * cf)[SUBLANES - 1:SUBLANES, :], (hb + pb * cbk)[0:1, :]]

    if final_only:
        for l in range(n_slab):
            o_ref[0:1, lanes[l]] = finals[2 * l]
            o_ref[1:2, lanes[l]] = finals[2 * l + 1]
        return

    def pass2(j, carry):
        jb = seg - 1 - j
        out = []
        for l in range(n_slab):
            hf, hb = carry[2 * l:2 * l + 2]
            hf = a_scr[l, tile(j), :] * hf + b_scr[l, tile(j), :]
            y_scr[l, tile(j), :] = hf
            hb = a_scr[n_slab + l, tile(jb), :] * hb + b_scr[n_slab + l, tile(jb), :]
            y_scr[n_slab + l, tile(jb), :] = hb
            out += [hf, hb]
        return tuple(out)

    lax.fori_loop(0, seg, pass2, tuple(starts), unroll=8)

    for s in range(SUBLANES):
        for l in range(n_slab):
            y = (y_scr[l, pl.ds(s, seg, stride=SUBLANES), :] + y_scr[n_slab + l, pl.ds(s, seg, stride=SUBLANES), :])
            g = g_ref[s * seg:(s + 1) * seg, lanes[l]].astype(F32)
            o_ref[s * seg:(s + 1) * seg, lanes[l]] = (_gelu_tanh(g) * y).astype(o_ref.dtype)


def _lru(p, conv_w, conv_b, wa, wx, ba, bx, lam, h0, *, batch, seq, col, final_only):
    nb, blk = wa.shape[1], wa.shape[2]
    w = nb * blk
    n_slab = blk // LANES
    assert seq % (2 * SUBLANES * SUBLANES) == 0 and seq % LRU_ROWS == 0
    xb = col["x_lru"] // blk
    in_specs = [
        pl.BlockSpec((seq, blk), lambda b, n: (b, xb + n)),
        pl.BlockSpec((conv_w.shape[0], blk), lambda b, n: (0, n)),
        pl.BlockSpec((1, blk), lambda b, n: (0, n)),
        pl.BlockSpec((2, None, blk, blk), lambda b, n: (0, n, 0, 0)),
        pl.BlockSpec((2, None, blk, blk), lambda b, n: (0, n, 0, 0)),
        pl.BlockSpec((2, blk), lambda b, n: (0, n)),
        pl.BlockSpec((2, blk), lambda b, n: (0, n)),
        pl.BlockSpec((2, blk), lambda b, n: (0, n)),
        pl.BlockSpec((None, 2, blk), lambda b, n: (b, 0, n)),
    ]
    args = [p, conv_w, conv_b.reshape(1, w), wa, wx, ba, bx, lam, h0]
    if final_only:
        out_spec = pl.BlockSpec((None, 2, blk), lambda b, n: (b, 0, n))
        out_shape = jax.ShapeDtypeStruct((batch, 2, w), F32)
    else:
        gb = col["g_lru"] // blk
        in_specs.append(pl.BlockSpec((seq, blk), lambda b, n: (b, gb + n)))
        args.append(p)
        out_spec = pl.BlockSpec((seq, blk), lambda b, n: (b, n))
        out_shape = jax.ShapeDtypeStruct((batch * seq, w), BF16)
    return pl.pallas_call(
        functools.partial(_lru_kernel, seq=seq, final_only=final_only),
        grid=(batch, nb),
        in_specs=in_specs,
        out_specs=out_spec,
        out_shape=out_shape,
        scratch_shapes=[
            pltpu.VMEM((n_slab, seq + SUBLANES * SUBLANES, LANES), F32),
            pltpu.VMEM((n_slab, seq + (conv_w.shape[0] - 1) * SUBLANES, LANES), F32),
        ] + [pltpu.VMEM((2 * n_slab, seq, LANES), F32)] * (2 if final_only else 3),
        compiler_params=_params(2),
        name="lru_ctx" if final_only else "lru",
    )(*args)


def _ffn_up_kernel(x_ref, wg_ref, wv_ref, cwg_ref, cwv_ref, cbg_ref, cbv_ref, o_ref):
    x = x_ref[...]
    t_idx = lax.broadcasted_iota(jnp.int32, o_ref.shape, 0)

    def conv(u, cw, cb):
        return cw[0:1] * _shift_rows(u, 1, t_idx) + cw[1:2] * u + cw[2:3] * _shift_rows(u, -1, t_idx) + cb

    g = conv(jnp.dot(x, wg_ref[...], preferred_element_type=F32), cwg_ref[...], cbg_ref[...])
    v = conv(jnp.dot(x, wv_ref[...], preferred_element_type=F32), cwv_ref[...], cbv_ref[...])
    o_ref[...] = (_silu(g) * v).astype(o_ref.dtype)


def _ffn_up(h, w_up, conv_w, conv_b, *, batch, seq, bn):
    d = h.shape[1]
    f = w_up.shape[1] // 2
    nj = f // bn
    cb = conv_b.reshape(1, 2 * f)
    taps = conv_w.shape[0]
    return pl.pallas_call(
        _ffn_up_kernel,
        grid=(batch, nj),
        in_specs=[
            pl.BlockSpec((seq, d), lambda b, j: (b, 0)),
            pl.BlockSpec((d, bn), lambda b, j: (0, j)),
            pl.BlockSpec((d, bn), lambda b, j: (0, nj + j)),
            pl.BlockSpec((taps, bn), lambda b, j: (0, j)),
            pl.BlockSpec((taps, bn), lambda b, j: (0, nj + j)),
            pl.BlockSpec((1, bn), lambda b, j: (0, j)),
            pl.BlockSpec((1, bn), lambda b, j: (0, nj + j)),
        ],
        out_specs=pl.BlockSpec((seq, bn), lambda b, j: (b, j)),
        out_shape=jax.ShapeDtypeStruct((batch * seq, f), BF16),
        compiler_params=_params(2),
        name="ffn_up",
    )(h, w_up, w_up, conv_w, conv_w, cb, cb)


def _tile(n, pref):
    return pref if n % pref == 0 else n


def kernel(x, c, ctx, c_ctx, w_ada, b_ada, norm1, norm2, w_in, ret_decay_logit, lru_conv_w, lru_conv_b,
           lru_wa, lru_ba, lru_wx, lru_bx, lru_lambda, w_ret_o, w_lru_o, w_out, w_up, ffn_conv_w,
           ffn_conv_b, w_down, final_norm):
    batch, seq, d = x.shape
    ctx_len = ctx.shape[1]
    assert w_in.shape[0] == 1, "single-layer trunk"
    assert batch + 1 <= ADA_ROWS
    n_heads = ret_decay_logit.shape[-1]
    qk_w, v_w, lru_w = n_heads * RET_QK_DIM, n_heads * RET_V_DIM, lru_lambda.shape[-1]
    sizes = (("k", qk_w), ("v", v_w), ("x_lru", lru_w), ("q", qk_w), ("g_ret", v_w), ("g_lru", lru_w),
             ("m_ret", d), ("m_lru", d))
    col, off = {}, 0
    for name, size in sizes:
        col[name] = off
        off += size
    state_cols = col["q"]
    m = batch * seq
    f = w_down.shape[1]

    bf = lambda a: a.astype(BF16)
    w_in_b, w_ret_b, w_lru_b, w_out_b = bf(w_in[0]), bf(w_ret_o[0]), bf(w_lru_o[0]), bf(w_out[0])
    w_up_b, w_down_b = bf(w_up[0]), bf(w_down[0])
    wa_b, wx_b = bf(0.5 * lru_wa[0]), bf(0.5 * lru_wx[0])

    c_rows = jnp.zeros((ADA_ROWS, d), F32).at[:batch].set(c).at[batch].set(c_ctx)
    mod = _ada(c_rows, w_ada[0], b_ada[0], tn=_tile(N_MOD * d, 512))
    sh1, sc1, g1, sh2, sc2, g2 = [mod[:batch, i * d:(i + 1) * d].reshape(batch, 1, d) for i in range(N_MOD)]
    csh, csc = [mod[batch, i * d:(i + 1) * d].reshape(1, 1, d) for i in range(2)]

    bm = _tile(seq, 1024)
    bn = 1024

    h_ctx = _norm_mod(ctx, norm1[0], csh, csc, tm=_tile(ctx_len, 256))
    pc = _matmul(h_ctx, w_in_b, bm=_tile(batch * ctx_len, 1024), bn=_tile(state_cols, bn), n_cols=state_cols,
                 out_dtype=BF16, name="w_in_ctx")
    lru_args = (lru_conv_w[0], lru_conv_b[0], wa_b, wx_b, 0.5 * lru_ba[0], 0.5 * lru_bx[0], lru_lambda[0])
    h0 = _lru(pc, *lru_args, jnp.zeros((batch, 2, lru_w), F32), batch=batch, seq=ctx_len, col=col,
              final_only=True)

    h_lat = _norm_mod(x, norm1[0], sh1, sc1, tm=_tile(seq, 256))
    p = _matmul(h_lat, w_in_b, bm=bm, bn=_tile(off, bn), out_dtype=BF16, name="w_in")
    cos2, sin2 = _rope_tables(seq)
    a_ret = _retention(p, pc, cos2, sin2, ret_decay_logit[0], batch=batch, seq=seq, ctx_len=ctx_len,
                       n_heads=n_heads, col=col)
    a_lru = _lru(p, *lru_args, h0, batch=batch, seq=seq, col=col, final_only=False)

    bnd = _tile(d, bn)
    gate_spec = lambda name: pl.BlockSpec((bm, bnd), lambda i, j, k: (i, col[name] // bnd + j))
    tile_spec = pl.BlockSpec((bm, bnd), lambda i, j, k: (i, j))
    y1 = _matmul(a_ret, w_ret_b, bm=bm, bn=bnd, out_dtype=BF16, epilogue=_ep_gate, extra=(p,),
                 extra_specs=(gate_spec("m_ret"),), name="w_ret_o")
    y = _matmul(a_lru, w_lru_b, bm=bm, bn=bnd, out_dtype=BF16, epilogue=_ep_gate_add, extra=(p, y1),
                extra_specs=(gate_spec("m_lru"), tile_spec), name="w_lru_o")
    steps_per_batch = seq // bm
    vec_spec = pl.BlockSpec((None, 1, bnd), lambda i, j, k: (i // steps_per_batch, 0, j))
    x2d = x.reshape(m, d)
    x_lat = _matmul(y, w_out_b, bm=bm, bn=bnd, out_dtype=F32, epilogue=_ep_resid, extra=(x2d, g1),
                    extra_specs=(tile_spec, vec_spec), name="w_out")

    h2 = _norm_mod(x_lat.reshape(batch, seq, d), norm2[0], sh2, sc2, tm=_tile(seq, 256))
    act = _ffn_up(h2, w_up_b, ffn_conv_w[0], ffn_conv_b[0], batch=batch, seq=seq, bn=_tile(f, 256))
    bk = f // 2 if (f // 2) % 128 == 0 else f
    bm_d = _tile(seq, 512)
    spb_d = seq // bm_d
    x_lat = _matmul(act, w_down_b, bm=bm_d, bn=bnd, bk=bk, out_dtype=F32, epilogue=_ep_resid,
                    extra=(x_lat, g2),
                    extra_specs=(pl.BlockSpec((bm_d, bnd), lambda i, j, k: (i, j)),
                                 pl.BlockSpec((None, 1, bnd), lambda i, j, k: (i // spb_d, 0, j))),
                    name="w_down")
    out = _rms(x_lat, final_norm, tm=_tile(m, 256))
    return out.reshape(batch, seq, d)
```

```python
import functools
import math

import jax
import jax.numpy as jnp
from jax import lax
from jax.experimental import pallas as pl
from jax.experimental.pallas import tpu as pltpu

F32 = jnp.float32
BF16 = jnp.bfloat16

EPS = 1e-6
N_MOD = 6
RET_QK_DIM = 128
RET_V_DIM = 256
RET_CHUNK = 128
GRID_W = 64
ROPE_BASE = 10000.0
LRU_C = 8.0
SUBLANES = 8
LANES = 128
ADA_ROWS = SUBLANES
LRU_ROWS = 128

V7X_VMEM_BYTES = 64 * 1024 * 1024
VMEM_LIMIT_BYTES = V7X_VMEM_BYTES - 4 * 1024 * 1024


def _params(n_axes):
    return pltpu.CompilerParams(
        dimension_semantics=("arbitrary",) * n_axes,
        vmem_limit_bytes=VMEM_LIMIT_BYTES,
    )


def _sigmoid(x):
    return 0.5 * jnp.tanh(0.5 * x) + 0.5


def _silu(x):
    return x * _sigmoid(x)


def _gelu_tanh(x):
    k = math.sqrt(2.0 / math.pi)
    half = 0.5 * x
    return half * jnp.tanh(x * ((k * 0.044715) * (x * x) + k)) + half


def _softplus(x):
    return jnp.maximum(x, 0.0) + jnp.log1p(jnp.exp(-jnp.abs(x)))


def _log_sigmoid(x):
    return -_softplus(-x)


def _shift_rows(x, k, t_idx):
    n = x.shape[0]
    rolled = pltpu.roll(x, k % n, 0)
    if k > 0:
        return jnp.where(t_idx >= k, rolled, 0.0)
    return jnp.where(t_idx < n + k, rolled, 0.0)


def _ada_kernel(c_ref, w_ref, b_ref, o_ref):
    s = _silu(c_ref[...]).astype(BF16)
    o_ref[...] = jnp.dot(s, w_ref[...].astype(BF16), preferred_element_type=F32) + b_ref[...]


def _ada(c_rows, w, b, *, tn):
    d, n = w.shape
    return pl.pallas_call(
        _ada_kernel,
        grid=(n // tn,),
        in_specs=[
            pl.BlockSpec((ADA_ROWS, d), lambda j: (0, 0)),
            pl.BlockSpec((d, tn), lambda j: (0, j)),
            pl.BlockSpec((1, tn), lambda j: (0, j)),
        ],
        out_specs=pl.BlockSpec((ADA_ROWS, tn), lambda j: (0, j)),
        out_shape=jax.ShapeDtypeStruct((ADA_ROWS, n), F32),
        compiler_params=_params(1),
        name="ada",
    )(c_rows, w, b.reshape(1, n))


def _norm_mod_kernel(x_ref, g_ref, sh_ref, sc_ref, o_ref):
    x = x_ref[...]
    y = x * lax.rsqrt(jnp.mean(x * x, axis=-1, keepdims=True) + EPS)
    o_ref[...] = ((y * g_ref[...]) * (1.0 + sc_ref[...]) + sh_ref[...]).astype(o_ref.dtype)


def _norm_mod(x, gain, shift, scale, *, tm):
    b, t, d = x.shape
    per_batch = shift.shape[0] == b
    mod_idx = (lambda bi, i: (bi, 0, 0)) if per_batch else (lambda bi, i: (0, 0, 0))
    nt = t // tm
    return pl.pallas_call(
        _norm_mod_kernel,
        grid=(b, nt),
        in_specs=[
            pl.BlockSpec((None, tm, d), lambda bi, i: (bi, i, 0)),
            pl.BlockSpec((1, d), lambda bi, i: (0, 0)),
            pl.BlockSpec((None, 1, d), mod_idx),
            pl.BlockSpec((None, 1, d), mod_idx),
        ],
        out_specs=pl.BlockSpec((tm, d), lambda bi, i: (bi * nt + i, 0)),
        out_shape=jax.ShapeDtypeStruct((b * t, d), BF16),
        compiler_params=_params(2),
        name="norm_mod",
    )(x, gain.reshape(1, d), shift, scale)


def _rms_kernel(x_ref, g_ref, o_ref):
    x = x_ref[...]
    y = x * lax.rsqrt(jnp.mean(x * x, axis=-1, keepdims=True) + EPS)
    o_ref[...] = y * g_ref[...]


def _rms(x, gain, *, tm):
    m, d = x.shape
    return pl.pallas_call(
        _rms_kernel,
        grid=(m // tm,),
        in_specs=[pl.BlockSpec((tm, d), lambda i: (i, 0)), pl.BlockSpec((1, d), lambda i: (0, 0))],
        out_specs=pl.BlockSpec((tm, d), lambda i: (i, 0)),
        out_shape=jax.ShapeDtypeStruct((m, d), F32),
        compiler_params=_params(1),
        name="final_rms",
    )(x, gain.reshape(1, d))


def _mm_kernel(x_ref, w_ref, *rest, epilogue, n_extra, nk):
    extra = rest[:n_extra]
    o_ref = rest[n_extra]
    if nk == 1:
        acc = jnp.dot(x_ref[...], w_ref[...].astype(BF16), preferred_element_type=F32)
        o_ref[...] = epilogue(acc, *extra).astype(o_ref.dtype)
        return
    acc_ref = rest[n_extra + 1]
    k = pl.program_id(2)

    @pl.when(k == 0)
    def _():
        acc_ref[...] = jnp.zeros_like(acc_ref)

    acc_ref[...] += jnp.dot(x_ref[...], w_ref[...].astype(BF16), preferred_element_type=F32)

    @pl.when(k == nk - 1)
    def _():
        o_ref[...] = epilogue(acc_ref[...], *extra).astype(o_ref.dtype)


def _matmul(x, w, *, bm, bn, bk=None, n_cols=None, out_dtype, epilogue=None, extra=(), extra_specs=(), name):
    m, kdim = x.shape
    n = w.shape[1] if n_cols is None else n_cols
    bk = kdim if bk is None else bk
    nk = kdim // bk
    assert m % bm == 0 and n % bn == 0 and kdim % bk == 0
    if epilogue is None:
        epilogue = lambda acc: acc
    kern = functools.partial(_mm_kernel, epilogue=epilogue, n_extra=len(extra), nk=nk)
    return pl.pallas_call(
        kern,
        grid=(m // bm, n // bn, nk),
        in_specs=[
            pl.BlockSpec((bm, bk), lambda i, j, k: (i, k)),
            pl.BlockSpec((bk, bn), lambda i, j, k: (k, j)),
            *extra_specs,
        ],
        out_specs=pl.BlockSpec((bm, bn), lambda i, j, k: (i, j)),
        out_shape=jax.ShapeDtypeStruct((m, n), out_dtype),
        scratch_shapes=[pltpu.VMEM((bm, bn), F32)] if nk > 1 else [],
        compiler_params=_params(3),
        name=name,
    )(x, w, *extra)


def _ep_gate(acc, m_ref):
    return _sigmoid(m_ref[...].astype(F32)) * acc


def _ep_gate_add(acc, m_ref, y_ref):
    return _sigmoid(m_ref[...].astype(F32)) * acc + y_ref[...].astype(F32)


def _ep_resid(acc, x_ref, g_ref):
    return x_ref[...] + g_ref[...] * acc


def _ret_kernel(q_ref, k_ref, v_ref, g_ref, kc_ref, vc_ref, cos_ref, sin_ref, dl_ref, o_ref,
                qs_ref, ks_ref, r_ref, *, seq, ctx_len):
    h = pl.program_id(1)
    n_heads = pl.num_programs(1)
    c = RET_CHUNK
    dk = RET_QK_DIM
    n_chunks = seq // c
    scale = dk ** -0.5
    tn_dims = (((0,), (0,)), ((), ()))
    nt_dims = (((1,), (1,)), ((), ()))

    lgf = _log_sigmoid(dl_ref[pl.ds(h, 1), :])
    lgb = _log_sigmoid(dl_ref[pl.ds(n_heads + h, 1), :])
    lgf_k = lgf[:, :dk]
    lgb_k = lgb[:, :dk]

    ri = lax.broadcasted_iota(jnp.int32, (c, c), 0).astype(F32)
    ci = lax.broadcasted_iota(jnp.int32, (c, c), 1).astype(F32)
    rel = ri - ci
    dmask = jnp.where(rel > 0, jnp.exp(jnp.maximum(rel, 0.0) * lgf_k),
                      jnp.where(rel < 0, jnp.exp(jnp.maximum(-rel, 0.0) * lgb_k), 2.0))
    qdf = jnp.exp((ri + 1.0) * lgf_k)
    qdb = jnp.exp((c - ri) * lgb_k)
    kdf = jnp.exp((c - 1.0 - ri) * lgf_k)
    kdb = jnp.exp(ri * lgb_k)
    cdf = jnp.exp(c * lgf)
    cdb = jnp.exp(c * lgb)

    li = lax.broadcasted_iota(jnp.int32, (ctx_len, dk), 0).astype(F32)
    kc = kc_ref[...].astype(F32) * scale
    vc = vc_ref[...]
    s0f = lax.dot_general((kc * jnp.exp((ctx_len - 1.0 - li) * lgf_k)).astype(BF16), vc, tn_dims,
                          preferred_element_type=F32)
    s0b = lax.dot_general((kc * jnp.exp(li * lgb_k)).astype(BF16), vc, tn_dims,
                          preferred_element_type=F32)

    cos = cos_ref[...]
    sin = sin_ref[...]
    q = q_ref[...].astype(F32)
    qs_ref[...] = (q * cos + pltpu.roll(q, dk // 2, 1) * sin).astype(BF16)
    k = k_ref[...].astype(F32)
    ks_ref[...] = ((k * cos + pltpu.roll(k, dk // 2, 1) * sin) * scale).astype(BF16)

    def rows(j):
        return pl.ds(pl.multiple_of(j * c, c), c)

    def bwd(i, r):
        j = n_chunks - 1 - i
        r_ref[j] = r.astype(BF16)
        kj = (ks_ref[rows(j), :].astype(F32) * kdb).astype(BF16)
        return cdb * r + lax.dot_general(kj, v_ref[rows(j), :], tn_dims, preferred_element_type=F32)

    lax.fori_loop(0, n_chunks, bwd, s0b, unroll=True)

    def fwd(j, s):
        qj = qs_ref[rows(j), :]
        kj = ks_ref[rows(j), :]
        vj = v_ref[rows(j), :]
        qk = lax.dot_general(qj, kj, nt_dims, preferred_element_type=F32)
        o = jnp.dot((qk * dmask).astype(BF16), vj, preferred_element_type=F32)
        qf = qj.astype(F32)
        qd = jnp.concatenate([(qf * qdf).astype(BF16), (qf * qdb).astype(BF16)], axis=1)
        st = jnp.concatenate([s.astype(BF16), r_ref[j]], axis=0)
        o = o + jnp.dot(qd, st, preferred_element_type=F32)
        mu = jnp.mean(o, axis=-1, keepdims=True)
        d = o - mu
        var = jnp.mean(d * d, axis=-1, keepdims=True)
        g = g_ref[rows(j), :].astype(F32)
        o_ref[rows(j), :] = (_silu(g) * (d * lax.rsqrt(var + EPS))).astype(o_ref.dtype)
        kd = (kj.astype(F32) * kdf).astype(BF16)
        return cdf * s + lax.dot_general(kd, vj, tn_dims, preferred_element_type=F32)

    lax.fori_loop(0, n_chunks, fwd, s0f, unroll=True)


def _retention(p, pc, cos2, sin2, decay_logit, *, batch, seq, ctx_len, n_heads, col):
    dk, dv = RET_QK_DIM, RET_V_DIM
    dl = jnp.broadcast_to(decay_logit.astype(F32).reshape(2 * n_heads, 1), (2 * n_heads, dv))
    kern = functools.partial(_ret_kernel, seq=seq, ctx_len=ctx_len)
    kb, vb, qb, gb = col["k"] // dk, col["v"] // dv, col["q"] // dk, col["g_ret"] // dv
    return pl.pallas_call(
        kern,
        grid=(batch, n_heads),
        in_specs=[
            pl.BlockSpec((seq, dk), lambda b, h: (b, qb + h)),
            pl.BlockSpec((seq, dk), lambda b, h: (b, kb + h)),
            pl.BlockSpec((seq, dv), lambda b, h: (b, vb + h)),
            pl.BlockSpec((seq, dv), lambda b, h: (b, gb + h)),
            pl.BlockSpec((ctx_len, dk), lambda b, h: (b, kb + h)),
            pl.BlockSpec((ctx_len, dv), lambda b, h: (b, vb + h)),
            pl.BlockSpec((seq, dk), lambda b, h: (0, 0)),
            pl.BlockSpec((seq, dk), lambda b, h: (0, 0)),
            pl.BlockSpec((2 * n_heads, dv), lambda b, h: (0, 0)),
        ],
        out_specs=pl.BlockSpec((seq, dv), lambda b, h: (b, h)),
        out_shape=jax.ShapeDtypeStruct((batch * seq, n_heads * dv), BF16),
        scratch_shapes=[
            pltpu.VMEM((seq, dk), BF16),
            pltpu.VMEM((seq, dk), BF16),
            pltpu.VMEM((seq // RET_CHUNK, dk, dv), BF16),
        ],
        compiler_params=_params(2),
        name="retention",
    )(p, p, p, p, pc, pc, cos2, sin2, dl)


def _rope_tables(seq):
    rows = seq // GRID_W
    row_ids = jnp.repeat(jnp.arange(rows, dtype=F32), GRID_W)
    col_ids = jnp.tile(jnp.arange(GRID_W, dtype=F32), rows)
    n_freq = RET_QK_DIM // 4
    inv_freq = ROPE_BASE ** (-jnp.arange(n_freq, dtype=F32) / n_freq)
    ang = jnp.concatenate([row_ids[:, None] * inv_freq, col_ids[:, None] * inv_freq], axis=-1)
    cos, sin = jnp.cos(ang), jnp.sin(ang)
    return jnp.concatenate([cos, cos], axis=-1), jnp.concatenate([-sin, sin], axis=-1)


def _lru_kernel(*refs, seq, final_only):
    if final_only:
        (x_ref, cw_ref, cb_ref, wa_ref, wx_ref, ba_ref, bx_ref, lam_ref, h0_ref, o_ref,
         xs_scr, xp_scr, a_scr, b_scr) = refs
        g_ref = y_scr = None
    else:
        (x_ref, cw_ref, cb_ref, wa_ref, wx_ref, ba_ref, bx_ref, lam_ref, h0_ref, g_ref, o_ref,
         xs_scr, xp_scr, a_scr, b_scr, y_scr) = refs
    blk = x_ref.shape[1]
    n_slab = blk // LANES
    seg = seq // SUBLANES
    pitch = seg + SUBLANES
    taps = cw_ref.shape[0]
    lead = taps - 2
    lanes = [slice(l * LANES, (l + 1) * LANES) for l in range(n_slab)]
    sub = lax.broadcasted_iota(jnp.int32, (SUBLANES, LANES), 0)

    for s in range(SUBLANES):
        xs = x_ref[s * seg:(s + 1) * seg, :].astype(F32)
        for l in range(n_slab):
            xs_scr[l, s * pitch:s * pitch + seg, :] = xs[:, lanes[l]]

    def tile(j):
        return pl.ds(pl.multiple_of(j * SUBLANES, SUBLANES), SUBLANES)

    def gather(j, carry):
        for l in range(n_slab):
            xp_scr[l, tile(j + lead), :] = xs_scr[l, pl.ds(j, SUBLANES, stride=pitch), :]
        return carry

    lax.fori_loop(0, seg, gather, 0, unroll=8)
    for l in range(n_slab):
        for k in range(lead):
            prev = xp_scr[l, (seg + k) * SUBLANES:(seg + k + 1) * SUBLANES, :]
            xp_scr[l, k * SUBLANES:(k + 1) * SUBLANES, :] = jnp.where(sub == 0, 0.0, pltpu.roll(prev, 1, 0))
        nxt = xp_scr[l, lead * SUBLANES:(lead + 1) * SUBLANES, :]
        xp_scr[l, (seg + lead) * SUBLANES:(seg + lead + 1) * SUBLANES, :] = jnp.where(
            sub == SUBLANES - 1, 0.0, pltpu.roll(nxt, SUBLANES - 1, 0))

    cw = cw_ref[...]
    cb = cb_ref[...]
    rate = [(0.25 * LRU_C) * _softplus(-lam_ref[z:z + 1, :]) for z in range(2)]

    def chunk(c, carry):
        r0 = pl.multiple_of(c * LRU_ROWS, LRU_ROWS)
        parts = []
        for l in range(n_slab):
            acc = cb[:, lanes[l]]
            for k in range(taps):
                acc = acc + cw[k:k + 1, lanes[l]] * xp_scr[l, pl.ds(r0 + k * SUBLANES, LRU_ROWS), :]
            parts.append(acc)
        xc = jnp.concatenate(parts, axis=1)
        xcb = xc.astype(BF16)
        for z in range(2):
            tr = jnp.tanh(jnp.dot(xcb, wa_ref[z], preferred_element_type=F32) + ba_ref[z:z + 1, :])
            ti = jnp.tanh(jnp.dot(xcb, wx_ref[z], preferred_element_type=F32) + bx_ref[z:z + 1, :])
            v = jnp.tanh(rate[z] * tr + rate[z])
            inv = 1.0 / (1.0 + v)
            a = (1.0 - v) * inv
            root = jnp.where(v > 0.0, v * lax.rsqrt(v), 0.0)
            b = (root * inv) * (xc * ti + xc)
            for l in range(n_slab):
                a_scr[z * n_slab + l, pl.ds(r0, LRU_ROWS), :] = a[:, lanes[l]]
                b_scr[z * n_slab + l, pl.ds(r0, LRU_ROWS), :] = b[:, lanes[l]]
        return carry

    lax.fori_loop(0, seq // LRU_ROWS, chunk, 0, unroll=8)

    def pass1(j, carry):
        jb = seg - 1 - j
        out = []
        for l in range(n_slab):
            hf, pf, hb, pb = carry[4 * l:4 * l + 4]
            af = a_scr[l, tile(j), :]
            ab = a_scr[n_slab + l, tile(jb), :]
            out += [af * hf + b_scr[l, tile(j), :], af * pf,
                    ab * hb + b_scr[n_slab + l, tile(jb), :], ab * pb]
        return tuple(out)

    zero = jnp.zeros((SUBLANES, LANES), F32)
    one = jnp.ones((SUBLANES, LANES), F32)
    ends = lax.fori_loop(0, seg, pass1, (zero, one, zero, one) * n_slab, unroll=8)

    h0 = h0_ref[...]
    starts = []
    finals = []
    for l in range(n_slab):
        hf, pf, hb, pb = ends[4 * l:4 * l + 4]
        h0f = jnp.broadcast_to(h0[0:1, lanes[l]], (SUBLANES, LANES))
        h0b = jnp.broadcast_to(h0[1:2, lanes[l]], (SUBLANES, LANES))
        cf, cbk = h0f, h0b
        for _ in range(SUBLANES - 1):
            cf = jnp.where(sub == 0, h0f, pltpu.roll(hf + pf * cf, 1, 0))
            cbk = jnp.where(sub == SUBLANES - 1, h0b, pltpu.roll(hb + pb * cbk, SUBLANES - 1, 0))
        starts += [cf, cbk]
        finals += [(hf + pf * cf)[SUBLANES - 1:SUBLANES, :], (hb + pb * cbk)[0:1, :]]

    if final_only:
        for l in range(n_slab):
            o_ref[0:1, lanes[l]] = finals[2 * l]
            o_ref[1:2, lanes[l]] = finals[2 * l + 1]
        return

    def pass2(j, carry):
        jb = seg - 1 - j
        out = []
        for l in range(n_slab):
            hf, hb = carry[2 * l:2 * l + 2]
            hf = a_scr[l, tile(j), :] * hf + b_scr[l, tile(j), :]
            y_scr[l, tile(j), :] = hf
            hb = a_scr[n_slab + l, tile(jb), :] * hb + b_scr[n_slab + l, tile(jb), :]
            y_scr[n_slab + l, tile(jb), :] = hb
            out += [hf, hb]
        return tuple(out)

    lax.fori_loop(0, seg, pass2, tuple(starts), unroll=8)

    for s in range(SUBLANES):
        for l in range(n_slab):
            y = (y_scr[l, pl.ds(s, seg, stride=SUBLANES), :] + y_scr[n_slab + l, pl.ds(s, seg, stride=SUBLANES), :])
            g = g_ref[s * seg:(s + 1) * seg, lanes[l]].astype(F32)
            o_ref[s * seg:(s + 1) * seg, lanes[l]] = (_gelu_tanh(g) * y).astype(o_ref.dtype)


def _lru(p, conv_w, conv_b, wa, wx, ba, bx, lam, h0, *, batch, seq, col, final_only):
    nb, blk = wa.shape[1], wa.shape[2]
    w = nb * blk
    n_slab = blk // LANES
    assert seq % (2 * SUBLANES * SUBLANES) == 0 and seq % LRU_ROWS == 0
    xb = col["x_lru"] // blk
    in_specs = [
        pl.BlockSpec((seq, blk), lambda b, n: (b, xb + n)),
        pl.BlockSpec((conv_w.shape[0], blk), lambda b, n: (0, n)),
        pl.BlockSpec((1, blk), lambda b, n: (0, n)),
        pl.BlockSpec((2, None, blk, blk), lambda b, n: (0, n, 0, 0)),
        pl.BlockSpec((2, None, blk, blk), lambda b, n: (0, n, 0, 0)),
        pl.BlockSpec((2, blk), lambda b, n: (0, n)),
        pl.BlockSpec((2, blk), lambda b, n: (0, n)),
        pl.BlockSpec((2, blk), lambda b, n: (0, n)),
        pl.BlockSpec((None, 2, blk), lambda b, n: (b, 0, n)),
    ]
    args = [p, conv_w, conv_b.reshape(1, w), wa, wx, ba, bx, lam, h0]
    if final_only:
        out_spec = pl.BlockSpec((None, 2, blk), lambda b, n: (b, 0, n))
        out_shape = jax.ShapeDtypeStruct((batch, 2, w), F32)
    else:
        gb = col["g_lru"] // blk
        in_specs.append(pl.BlockSpec((seq, blk), lambda b, n: (b, gb + n)))
        args.append(p)
        out_spec = pl.BlockSpec((seq, blk), lambda b, n: (b, n))
        out_shape = jax.ShapeDtypeStruct((batch * seq, w), BF16)
    return pl.pallas_call(
        functools.partial(_lru_kernel, seq=seq, final_only=final_only),
        grid=(batch, nb),
        in_specs=in_specs,
        out_specs=out_spec,
        out_shape=out_shape,
        scratch_shapes=[
            pltpu.VMEM((n_slab, seq + SUBLANES * SUBLANES, LANES), F32),
            pltpu.VMEM((n_slab, seq + (conv_w.shape[0] - 1) * SUBLANES, LANES), F32),
        ] + [pltpu.VMEM((2 * n_slab, seq, LANES), F32)] * (2 if final_only else 3),
        compiler_params=_params(2),
        name="lru_ctx" if final_only else "lru",
    )(*args)


def _ffn_up_kernel(x_ref, wg_ref, wv_ref, cwg_ref, cwv_ref, cbg_ref, cbv_ref, o_ref):
    x = x_ref[...]
    t_idx = lax.broadcasted_iota(jnp.int32, o_ref.shape, 0)

    def conv(u, cw, cb):
        return cw[0:1] * _shift_rows(u, 1, t_idx) + cw[1:2] * u + cw[2:3] * _shift_rows(u, -1, t_idx) + cb

    g = conv(jnp.dot(x, wg_ref[...].astype(BF16), preferred_element_type=F32), cwg_ref[...], cbg_ref[...])
    v = conv(jnp.dot(x, wv_ref[...].astype(BF16), preferred_element_type=F32), cwv_ref[...], cbv_ref[...])
    o_ref[...] = (_silu(g) * v).astype(o_ref.dtype)


def _ffn_up(h, w_up, conv_w, conv_b, *, batch, seq, bn):
    d = h.shape[1]
    f = w_up.shape[1] // 2
    nj = f // bn
    cb = conv_b.reshape(1, 2 * f)
    taps = conv_w.shape[0]
    return pl.pallas_call(
        _ffn_up_kernel,
        grid=(batch, nj),
        in_specs=[
            pl.BlockSpec((seq, d), lambda b, j: (b, 0)),
            pl.BlockSpec((d, bn), lambda b, j: (0, j)),
            pl.BlockSpec((d, bn), lambda b, j: (0, nj + j)),
            pl.BlockSpec((taps, bn), lambda b, j: (0, j)),
            pl.BlockSpec((taps, bn), lambda b, j: (0, nj + j)),
            pl.BlockSpec((1, bn), lambda b, j: (0, j)),
            pl.BlockSpec((1, bn), lambda b, j: (0, nj + j)),
        ],
        out_specs=pl.BlockSpec((seq, bn), lambda b, j: (b, j)),
        out_shape=jax.ShapeDtypeStruct((batch * seq, f), BF16),
        compiler_params=_params(2),
        name="ffn_up",
    )(h, w_up, w_up, conv_w, conv_w, cb, cb)


def _tile(n, pref):
    return pref if n % pref == 0 else n


def kernel(x, c, ctx, c_ctx, w_ada, b_ada, norm1, norm2, w_in, ret_decay_logit, lru_conv_w, lru_conv_b,
           lru_wa, lru_ba, lru_wx, lru_bx, lru_lambda, w_ret_o, w_lru_o, w_out, w_up, ffn_conv_w,
           ffn_conv_b, w_down, final_norm):
    batch, seq, d = x.shape
    ctx_len = ctx.shape[1]
    assert w_in.shape[0] == 1, "single-layer trunk"
    assert batch + 1 <= ADA_ROWS
    n_heads = ret_decay_logit.shape[-1]
    qk_w, v_w, lru_w = n_heads * RET_QK_DIM, n_heads * RET_V_DIM, lru_lambda.shape[-1]
    sizes = (("k", qk_w), ("v", v_w), ("x_lru", lru_w), ("q", qk_w), ("g_ret", v_w), ("g_lru", lru_w),
             ("m_ret", d), ("m_lru", d))
    col, off = {}, 0
    for name, size in sizes:
        col[name] = off
        off += size
    state_cols = col["q"]
    m = batch * seq
    f = w_down.shape[1]

    bf = lambda a: a.astype(BF16)
    w_down_b = bf(w_down[0])
    wa_b, wx_b = bf(0.5 * lru_wa[0]), bf(0.5 * lru_wx[0])

    c_rows = jnp.zeros((ADA_ROWS, d), F32).at[:batch].set(c).at[batch].set(c_ctx)
    mod = _ada(c_rows, w_ada[0], b_ada[0], tn=_tile(N_MOD * d, 512))
    sh1, sc1, g1, sh2, sc2, g2 = [mod[:batch, i * d:(i + 1) * d].reshape(batch, 1, d) for i in range(N_MOD)]
    csh, csc = [mod[batch, i * d:(i + 1) * d].reshape(1, 1, d) for i in range(2)]

    bm = _tile(seq, 1024)
    bn = 512

    h_ctx = _norm_mod(ctx, norm1[0], csh, csc, tm=_tile(ctx_len, 256))
    pc = _matmul(h_ctx, w_in[0], bm=_tile(batch * ctx_len, 1024), bn=_tile(state_cols, bn), n_cols=state_cols,
                 out_dtype=BF16, name="w_in_ctx")
    lru_args = (lru_conv_w[0], lru_conv_b[0], wa_b, wx_b, 0.5 * lru_ba[0], 0.5 * lru_bx[0], lru_lambda[0])
    h0 = _lru(pc, *lru_args, jnp.zeros((batch, 2, lru_w), F32), batch=batch, seq=ctx_len, col=col,
              final_only=True)

    h_lat = _norm_mod(x, norm1[0], sh1, sc1, tm=_tile(seq, 512))
    p = _matmul(h_lat, w_in[0], bm=bm, bn=_tile(off, bn), out_dtype=BF16, name="w_in")
    cos2, sin2 = _rope_tables(seq)
    a_ret = _retention(p, pc, cos2, sin2, ret_decay_logit[0], batch=batch, seq=seq, ctx_len=ctx_len,
                       n_heads=n_heads, col=col)
    a_lru = _lru(p, *lru_args, h0, batch=batch, seq=seq, col=col, final_only=False)

    bnd = _tile(d, bn)
    gate_spec = lambda name: pl.BlockSpec((bm, bnd), lambda i, j, k: (i, col[name] // bnd + j))
    tile_spec = pl.BlockSpec((bm, bnd), lambda i, j, k: (i, j))
    y1 = _matmul(a_ret, w_ret_o[0], bm=bm, bn=bnd, out_dtype=BF16, epilogue=_ep_gate, extra=(p,),
                 extra_specs=(gate_spec("m_ret"),), name="w_ret_o")
    y = _matmul(a_lru, w_lru_o[0], bm=bm, bn=bnd, out_dtype=BF16, epilogue=_ep_gate_add, extra=(p, y1),
                extra_specs=(gate_spec("m_lru"), tile_spec), name="w_lru_o")
    steps_per_batch = seq // bm
    vec_spec = pl.BlockSpec((None, 1, bnd), lambda i, j, k: (i // steps_per_batch, 0, j))
    x2d = x.reshape(m, d)
    x_lat = _matmul(y, w_out[0], bm=bm, bn=bnd, out_dtype=F32, epilogue=_ep_resid, extra=(x2d, g1),
                    extra_specs=(tile_spec, vec_spec), name="w_out")

    h2 = _norm_mod(x_lat.reshape(batch, seq, d), norm2[0], sh2, sc2, tm=_tile(seq, 512))
    act = _ffn_up(h2, w_up[0], ffn_conv_w[0], ffn_conv_b[0], batch=batch, seq=seq, bn=_tile(f, 256))
    bk = f // 2 if (f // 2) % 128 == 0 else f
    bm_d = _tile(seq, 512)
    bn_d = _tile(d, 1024)
    spb_d = seq // bm_d
    x_lat = _matmul(act, w_down_b, bm=bm_d, bn=bn_d, bk=bk, out_dtype=F32, epilogue=_ep_resid,
                    extra=(x_lat, g2),
                    extra_specs=(pl.BlockSpec((bm_d, bn_d), lambda i, j, k: (i, j)),
                                 pl.BlockSpec((None, 1, bn_d), lambda i, j, k: (i // spb_d, 0, j))),
                    name="w_down")
    out = _rms(x_lat, final_norm, tm=_tile(m, 512))
    return out.reshape(batch, seq, d)
```

```python
import functools
import math

import jax
import jax.numpy as jnp
from jax import lax
from jax.experimental import pallas as pl
from jax.experimental.pallas import tpu as pltpu

F32 = jnp.float32
BF16 = jnp.bfloat16

EPS = 1e-6
N_MOD = 6
RET_QK_DIM = 128
RET_V_DIM = 256
RET_CHUNK = 128
GRID_W = 64
ROPE_BASE = 10000.0
LRU_C = 8.0
SUBLANES = 8
LANES = 128
ADA_ROWS = SUBLANES
LRU_ROWS = 128

V7X_VMEM_BYTES = 64 * 1024 * 1024
VMEM_LIMIT_BYTES = V7X_VMEM_BYTES - 4 * 1024 * 1024


def _params(n_axes):
    return pltpu.CompilerParams(
        dimension_semantics=("arbitrary",) * n_axes,
        vmem_limit_bytes=VMEM_LIMIT_BYTES,
    )


def _sigmoid(x):
    return 0.5 * jnp.tanh(0.5 * x) + 0.5


def _silu(x):
    return x * _sigmoid(x)


def _gelu_tanh(x):
    k = math.sqrt(2.0 / math.pi)
    half = 0.5 * x
    return half * jnp.tanh(x * ((k * 0.044715) * (x * x) + k)) + half


def _softplus(x):
    return jnp.maximum(x, 0.0) + jnp.log1p(jnp.exp(-jnp.abs(x)))


def _log_sigmoid(x):
    return -_softplus(-x)


def _shift_rows(x, k, t_idx):
    n = x.shape[0]
    rolled = pltpu.roll(x, k % n, 0)
    if k > 0:
        return jnp.where(t_idx >= k, rolled, 0.0)
    return jnp.where(t_idx < n + k, rolled, 0.0)


def _ada_kernel(c_ref, w_ref, b_ref, o_ref):
    s = _silu(c_ref[...]).astype(BF16)
    o_ref[...] = jnp.dot(s, w_ref[...].astype(BF16), preferred_element_type=F32) + b_ref[...]


def _ada(c_rows, w, b, *, tn):
    d, n = w.shape
    return pl.pallas_call(
        _ada_kernel,
        grid=(n // tn,),
        in_specs=[
            pl.BlockSpec((ADA_ROWS, d), lambda j: (0, 0)),
            pl.BlockSpec((d, tn), lambda j: (0, j)),
            pl.BlockSpec((1, tn), lambda j: (0, j)),
        ],
        out_specs=pl.BlockSpec((ADA_ROWS, tn), lambda j: (0, j)),
        out_shape=jax.ShapeDtypeStruct((ADA_ROWS, n), F32),
        compiler_params=_params(1),
        name="ada",
    )(c_rows, w, b.reshape(1, n))


def _norm_mod_kernel(x_ref, g_ref, sh_ref, sc_ref, o_ref):
    x = x_ref[...]
    y = x * lax.rsqrt(jnp.mean(x * x, axis=-1, keepdims=True) + EPS)
    o_ref[...] = ((y * g_ref[...]) * (1.0 + sc_ref[...]) + sh_ref[...]).astype(o_ref.dtype)


def _norm_mod(x, gain, shift, scale, *, tm):
    b, t, d = x.shape
    per_batch = shift.shape[0] == b
    mod_idx = (lambda bi, i: (bi, 0, 0)) if per_batch else (lambda bi, i: (0, 0, 0))
    nt = t // tm
    return pl.pallas_call(
        _norm_mod_kernel,
        grid=(b, nt),
        in_specs=[
            pl.BlockSpec((None, tm, d), lambda bi, i: (bi, i, 0)),
            pl.BlockSpec((1, d), lambda bi, i: (0, 0)),
            pl.BlockSpec((None, 1, d), mod_idx),
            pl.BlockSpec((None, 1, d), mod_idx),
        ],
        out_specs=pl.BlockSpec((tm, d), lambda bi, i: (bi * nt + i, 0)),
        out_shape=jax.ShapeDtypeStruct((b * t, d), BF16),
        compiler_params=_params(2),
        name="norm_mod",
    )(x, gain.reshape(1, d), shift, scale)


def _down_kernel(x_ref, w_ref, r_ref, g_ref, gain_ref, o_ref, *, bn, nj):
    j = pl.program_id(1)
    acc = jnp.dot(x_ref[...], w_ref[...], preferred_element_type=F32)
    o_ref[:, pl.ds(pl.multiple_of(j * bn, bn), bn)] = r_ref[...] + g_ref[...] * acc

    @pl.when(j == nj - 1)
    def _():
        y = o_ref[...]
        o_ref[...] = y * lax.rsqrt(jnp.mean(y * y, axis=-1, keepdims=True) + EPS) * gain_ref[...]


def _down_rms(act, w, resid, gate, gain, *, seq, bm, bn):
    m, k = act.shape
    d = w.shape[1]
    nj = d // bn
    steps_per_batch = seq // bm
    return pl.pallas_call(
        functools.partial(_down_kernel, bn=bn, nj=nj),
        grid=(m // bm, nj),
        in_specs=[
            pl.BlockSpec((bm, k), lambda i, j: (i, 0)),
            pl.BlockSpec((k, bn), lambda i, j: (0, j)),
            pl.BlockSpec((bm, bn), lambda i, j: (i, j)),
            pl.BlockSpec((None, 1, bn), lambda i, j: (i // steps_per_batch, 0, j)),
            pl.BlockSpec((1, d), lambda i, j: (0, 0)),
        ],
        out_specs=pl.BlockSpec((bm, d), lambda i, j: (i, 0)),
        out_shape=jax.ShapeDtypeStruct((m, d), F32),
        compiler_params=_params(2),
        name="w_down_rms",
    )(act, w, resid, gate, gain.reshape(1, d))


def _mm_kernel(x_ref, w_ref, *rest, epilogue):
    *extra, o_ref = rest
    acc = jnp.dot(x_ref[...], w_ref[...].astype(BF16), preferred_element_type=F32)
    o_ref[...] = epilogue(acc, *extra).astype(o_ref.dtype)


def _matmul(x, w, *, bm, bn, n_cols=None, out_dtype, epilogue=None, extra=(), extra_specs=(), name):
    m, kdim = x.shape
    n = w.shape[1] if n_cols is None else n_cols
    assert m % bm == 0 and n % bn == 0
    if epilogue is None:
        epilogue = lambda acc: acc
    return pl.pallas_call(
        functools.partial(_mm_kernel, epilogue=epilogue),
        grid=(m // bm, n // bn),
        in_specs=[
            pl.BlockSpec((bm, kdim), lambda i, j: (i, 0)),
            pl.BlockSpec((kdim, bn), lambda i, j: (0, j)),
            *extra_specs,
        ],
        out_specs=pl.BlockSpec((bm, bn), lambda i, j: (i, j)),
        out_shape=jax.ShapeDtypeStruct((m, n), out_dtype),
        compiler_params=_params(2),
        name=name,
    )(x, w, *extra)


def _ep_gate(acc, m_ref):
    return _sigmoid(m_ref[...].astype(F32)) * acc


def _ep_gate_add(acc, m_ref, y_ref):
    return _sigmoid(m_ref[...].astype(F32)) * acc + y_ref[...].astype(F32)


def _ep_resid(acc, x_ref, g_ref):
    return x_ref[...] + g_ref[...] * acc


def _ret_kernel(q_ref, k_ref, v_ref, g_ref, kc_ref, vc_ref, cos_ref, sin_ref, dl_ref, o_ref,
                qs_ref, ks_ref, r_ref, *, seq, ctx_len):
    h = pl.program_id(1)
    n_heads = pl.num_programs(1)
    c = RET_CHUNK
    dk = RET_QK_DIM
    n_chunks = seq // c
    scale = dk ** -0.5
    tn_dims = (((0,), (0,)), ((), ()))
    nt_dims = (((1,), (1,)), ((), ()))

    lgf = _log_sigmoid(dl_ref[pl.ds(h, 1), :])
    lgb = _log_sigmoid(dl_ref[pl.ds(n_heads + h, 1), :])
    lgf_k = lgf[:, :dk]
    lgb_k = lgb[:, :dk]

    ri = lax.broadcasted_iota(jnp.int32, (c, c), 0).astype(F32)
    ci = lax.broadcasted_iota(jnp.int32, (c, c), 1).astype(F32)
    rel = ri - ci
    dmask = jnp.where(rel > 0, jnp.exp(jnp.maximum(rel, 0.0) * lgf_k),
                      jnp.where(rel < 0, jnp.exp(jnp.maximum(-rel, 0.0) * lgb_k), 2.0))
    qdf = jnp.exp((ri + 1.0) * lgf_k)
    qdb = jnp.exp((c - ri) * lgb_k)
    kdf = jnp.exp((c - 1.0 - ri) * lgf_k)
    kdb = jnp.exp(ri * lgb_k)
    cdf = jnp.exp(c * lgf)
    cdb = jnp.exp(c * lgb)

    li = lax.broadcasted_iota(jnp.int32, (ctx_len, dk), 0).astype(F32)
    kc = kc_ref[...].astype(F32) * scale
    vc = vc_ref[...]
    s0f = lax.dot_general((kc * jnp.exp((ctx_len - 1.0 - li) * lgf_k)).astype(BF16), vc, tn_dims,
                          preferred_element_type=F32)
    s0b = lax.dot_general((kc * jnp.exp(li * lgb_k)).astype(BF16), vc, tn_dims,
                          preferred_element_type=F32)

    cos = cos_ref[...]
    sin = sin_ref[...]
    q = q_ref[...].astype(F32)
    qs_ref[...] = (q * cos + pltpu.roll(q, dk // 2, 1) * sin).astype(BF16)
    k = k_ref[...].astype(F32)
    ks_ref[...] = ((k * cos + pltpu.roll(k, dk // 2, 1) * sin) * scale).astype(BF16)

    def rows(j):
        return pl.ds(pl.multiple_of(j * c, c), c)

    def bwd(i, r):
        j = n_chunks - 1 - i
        r_ref[j] = r.astype(BF16)
        kj = (ks_ref[rows(j), :].astype(F32) * kdb).astype(BF16)
        return cdb * r + lax.dot_general(kj, v_ref[rows(j), :], tn_dims, preferred_element_type=F32)

    lax.fori_loop(0, n_chunks, bwd, s0b, unroll=True)

    def fwd(j, s):
        qj = qs_ref[rows(j), :]
        kj = ks_ref[rows(j), :]
        vj = v_ref[rows(j), :]
        qk = lax.dot_general(qj, kj, nt_dims, preferred_element_type=F32)
        o = jnp.dot((qk * dmask).astype(BF16), vj, preferred_element_type=F32)
        qf = qj.astype(F32)
        qd = jnp.concatenate([(qf * qdf).astype(BF16), (qf * qdb).astype(BF16)], axis=1)
        st = jnp.concatenate([s.astype(BF16), r_ref[j]], axis=0)
        o = o + jnp.dot(qd, st, preferred_element_type=F32)
        mu = jnp.mean(o, axis=-1, keepdims=True)
        d = o - mu
        var = jnp.mean(d * d, axis=-1, keepdims=True)
        g = g_ref[rows(j), :].astype(F32)
        o_ref[rows(j), :] = (_silu(g) * (d * lax.rsqrt(var + EPS))).astype(o_ref.dtype)
        kd = (kj.astype(F32) * kdf).astype(BF16)
        return cdf * s + lax.dot_general(kd, vj, tn_dims, preferred_element_type=F32)

    lax.fori_loop(0, n_chunks, fwd, s0f, unroll=True)


def _retention(p, pc, cos2, sin2, decay_logit, *, batch, seq, ctx_len, n_heads, col):
    dk, dv = RET_QK_DIM, RET_V_DIM
    dl = jnp.broadcast_to(decay_logit.astype(F32).reshape(2 * n_heads, 1), (2 * n_heads, dv))
    kern = functools.partial(_ret_kernel, seq=seq, ctx_len=ctx_len)
    kb, vb, qb, gb = col["k"] // dk, col["v"] // dv, col["q"] // dk, col["g_ret"] // dv
    return pl.pallas_call(
        kern,
        grid=(batch, n_heads),
        in_specs=[
            pl.BlockSpec((seq, dk), lambda b, h: (b, qb + h)),
            pl.BlockSpec((seq, dk), lambda b, h: (b, kb + h)),
            pl.BlockSpec((seq, dv), lambda b, h: (b, vb + h)),
            pl.BlockSpec((seq, dv), lambda b, h: (b, gb + h)),
            pl.BlockSpec((ctx_len, dk), lambda b, h: (b, kb + h)),
            pl.BlockSpec((ctx_len, dv), lambda b, h: (b, vb + h)),
            pl.BlockSpec((seq, dk), lambda b, h: (0, 0)),
            pl.BlockSpec((seq, dk), lambda b, h: (0, 0)),
            pl.BlockSpec((2 * n_heads, dv), lambda b, h: (0, 0)),
        ],
        out_specs=pl.BlockSpec((seq, dv), lambda b, h: (b, h)),
        out_shape=jax.ShapeDtypeStruct((batch * seq, n_heads * dv), BF16),
        scratch_shapes=[
            pltpu.VMEM((seq, dk), BF16),
            pltpu.VMEM((seq, dk), BF16),
            pltpu.VMEM((seq // RET_CHUNK, dk, dv), BF16),
        ],
        compiler_params=_params(2),
        name="retention",
    )(p, p, p, p, pc, pc, cos2, sin2, dl)


def _rope_tables(seq):
    rows = seq // GRID_W
    row_ids = jnp.repeat(jnp.arange(rows, dtype=F32), GRID_W)
    col_ids = jnp.tile(jnp.arange(GRID_W, dtype=F32), rows)
    n_freq = RET_QK_DIM // 4
    inv_freq = ROPE_BASE ** (-jnp.arange(n_freq, dtype=F32) / n_freq)
    ang = jnp.concatenate([row_ids[:, None] * inv_freq, col_ids[:, None] * inv_freq], axis=-1)
    cos, sin = jnp.cos(ang), jnp.sin(ang)
    return jnp.concatenate([cos, cos], axis=-1), jnp.concatenate([-sin, sin], axis=-1)


def _lru_kernel(*refs, seq, final_only):
    if final_only:
        (x_ref, cw_ref, cb_ref, wa_ref, wx_ref, ba_ref, bx_ref, lam_ref, h0_ref, o_ref,
         xs_scr, xp_scr, a_scr, b_scr) = refs
        g_ref = y_scr = None
    else:
        (x_ref, cw_ref, cb_ref, wa_ref, wx_ref, ba_ref, bx_ref, lam_ref, h0_ref, g_ref, o_ref,
         xs_scr, xp_scr, a_scr, b_scr, y_scr) = refs
    blk = x_ref.shape[1]
    n_slab = blk // LANES
    seg = seq // SUBLANES
    pitch = seg + SUBLANES
    taps = cw_ref.shape[0]
    lead = taps - 2
    lanes = [slice(l * LANES, (l + 1) * LANES) for l in range(n_slab)]
    sub = lax.broadcasted_iota(jnp.int32, (SUBLANES, LANES), 0)

    for s in range(SUBLANES):
        xs = x_ref[s * seg:(s + 1) * seg, :].astype(F32)
        for l in range(n_slab):
            xs_scr[l, s * pitch:s * pitch + seg, :] = xs[:, lanes[l]]

    def tile(j):
        return pl.ds(pl.multiple_of(j * SUBLANES, SUBLANES), SUBLANES)

    def gather(j, carry):
        for l in range(n_slab):
            xp_scr[l, tile(j + lead), :] = xs_scr[l, pl.ds(j, SUBLANES, stride=pitch), :]
        return carry

    lax.fori_loop(0, seg, gather, 0, unroll=8)
    for l in range(n_slab):
        for k in range(lead):
            prev = xp_scr[l, (seg + k) * SUBLANES:(seg + k + 1) * SUBLANES, :]
            xp_scr[l, k * SUBLANES:(k + 1) * SUBLANES, :] = jnp.where(sub == 0, 0.0, pltpu.roll(prev, 1, 0))
        nxt = xp_scr[l, lead * SUBLANES:(lead + 1) * SUBLANES, :]
        xp_scr[l, (seg + lead) * SUBLANES:(seg + lead + 1) * SUBLANES, :] = jnp.where(
            sub == SUBLANES - 1, 0.0, pltpu.roll(nxt, SUBLANES - 1, 0))

    cw = cw_ref[...]
    cb = cb_ref[...]
    rate = [(0.25 * LRU_C) * _softplus(-lam_ref[z:z + 1, :]) for z in range(2)]

    def chunk(c, carry):
        r0 = pl.multiple_of(c * LRU_ROWS, LRU_ROWS)
        parts = []
        for l in range(n_slab):
            acc = cb[:, lanes[l]]
            for k in range(taps):
                acc = acc + cw[k:k + 1, lanes[l]] * xp_scr[l, pl.ds(r0 + k * SUBLANES, LRU_ROWS), :]
            parts.append(acc)
        xc = jnp.concatenate(parts, axis=1)
        xcb = xc.astype(BF16)
        for z in range(2):
            tr = jnp.tanh(jnp.dot(xcb, wa_ref[z], preferred_element_type=F32) + ba_ref[z:z + 1, :])
            ti = jnp.tanh(jnp.dot(xcb, wx_ref[z], preferred_element_type=F32) + bx_ref[z:z + 1, :])
            v = jnp.tanh(rate[z] * tr + rate[z])
            inv = 1.0 / (1.0 + v)
            a = (1.0 - v) * inv
            root = jnp.where(v > 0.0, v * lax.rsqrt(v), 0.0)
            b = (root * inv) * (xc * ti + xc)
            for l in range(n_slab):
                a_scr[z * n_slab + l, pl.ds(r0, LRU_ROWS), :] = a[:, lanes[l]]
                b_scr[z * n_slab + l, pl.ds(r0, LRU_ROWS), :] = b[:, lanes[l]]
        return carry

    lax.fori_loop(0, seq // LRU_ROWS, chunk, 0, unroll=8)

    def pass1(j, carry):
        jb = seg - 1 - j
        out = []
        for l in range(n_slab):
            hf, pf, hb, pb = carry[4 * l:4 * l + 4]
            af = a_scr[l, tile(j), :]
            ab = a_scr[n_slab + l, tile(jb), :]
            out += [af * hf + b_scr[l, tile(j), :], af * pf,
                    ab * hb + b_scr[n_slab + l, tile(jb), :], ab * pb]
        return tuple(out)

    zero = jnp.zeros((SUBLANES, LANES), F32)
    one = jnp.ones((SUBLANES, LANES), F32)
    ends = lax.fori_loop(0, seg, pass1, (zero, one, zero, one) * n_slab, unroll=8)

    h0 = h0_ref[...]
    starts = []
    finals = []
    for l in range(n_slab):
        hf, pf, hb, pb = ends[4 * l:4 * l + 4]
        h0f = jnp.broadcast_to(h0[0:1, lanes[l]], (SUBLANES, LANES))
        h0b = jnp.broadcast_to(h0[1:2, lanes[l]], (SUBLANES, LANES))
        cf, cbk = h0f, h0b
        for _ in range(SUBLANES - 1):
            cf = jnp.where(sub == 0, h0f, pltpu.roll(hf + pf * cf, 1, 0))
            cbk = jnp.where(sub == SUBLANES - 1, h0b, pltpu.roll(hb + pb * cbk, SUBLANES - 1, 0))
        starts += [cf, cbk]
        finals += [(hf + pf * cf)[SUBLANES - 1:SUBLANES, :], (hb + pb * cbk)[0:1, :]]

    if final_only:
        for l in range(n_slab):
            o_ref[0:1, lanes[l]] = finals[2 * l]
            o_ref[1:2, lanes[l]] = finals[2 * l + 1]
        return

    def pass2(j, carry):
        jb = seg - 1 - j
        out = []
        for l in range(n_slab):
            hf, hb = carry[2 * l:2 * l + 2]
            hf = a_scr[l, tile(j), :] * hf + b_scr[l, tile(j), :]
            y_scr[l, tile(j), :] = hf
            hb = a_scr[n_slab + l, tile(jb), :] * hb + b_scr[n_slab + l, tile(jb), :]
            y_scr[n_slab + l, tile(jb), :] = hb
            out += [hf, hb]
        return tuple(out)

    lax.fori_loop(0, seg, pass2, tuple(starts), unroll=8)

    for s in range(SUBLANES):
        for l in range(n_slab):
            y = (y_scr[l, pl.ds(s, seg, stride=SUBLANES), :] + y_scr[n_slab + l, pl.ds(s, seg, stride=SUBLANES), :])
            g = g_ref[s * seg:(s + 1) * seg, lanes[l]].astype(F32)
            o_ref[s * seg:(s + 1) * seg, lanes[l]] = (_gelu_tanh(g) * y).astype(o_ref.dtype)


def _lru(p, conv_w, conv_b, wa, wx, ba, bx, lam, h0, *, batch, seq, col, final_only):
    nb, blk = wa.shape[1], wa.shape[2]
    w = nb * blk
    n_slab = blk // LANES
    assert seq % (2 * SUBLANES * SUBLANES) == 0 and seq % LRU_ROWS == 0
    xb = col["x_lru"] // blk
    in_specs = [
        pl.BlockSpec((seq, blk), lambda b, n: (b, xb + n)),
        pl.BlockSpec((conv_w.shape[0], blk), lambda b, n: (0, n)),
        pl.BlockSpec((1, blk), lambda b, n: (0, n)),
        pl.BlockSpec((2, None, blk, blk), lambda b, n: (0, n, 0, 0)),
        pl.BlockSpec((2, None, blk, blk), lambda b, n: (0, n, 0, 0)),
        pl.BlockSpec((2, blk), lambda b, n: (0, n)),
        pl.BlockSpec((2, blk), lambda b, n: (0, n)),
        pl.BlockSpec((2, blk), lambda b, n: (0, n)),
        pl.BlockSpec((None, 2, blk), lambda b, n: (b, 0, n)),
    ]
    args = [p, conv_w, conv_b.reshape(1, w), wa, wx, ba, bx, lam, h0]
    if final_only:
        out_spec = pl.BlockSpec((None, 2, blk), lambda b, n: (b, 0, n))
        out_shape = jax.ShapeDtypeStruct((batch, 2, w), F32)
    else:
        gb = col["g_lru"] // blk
        in_specs.append(pl.BlockSpec((seq, blk), lambda b, n: (b, gb + n)))
        args.append(p)
        out_spec = pl.BlockSpec((seq, blk), lambda b, n: (b, n))
        out_shape = jax.ShapeDtypeStruct((batch * seq, w), BF16)
    return pl.pallas_call(
        functools.partial(_lru_kernel, seq=seq, final_only=final_only),
        grid=(batch, nb),
        in_specs=in_specs,
        out_specs=out_spec,
        out_shape=out_shape,
        scratch_shapes=[
            pltpu.VMEM((n_slab, seq + SUBLANES * SUBLANES, LANES), F32),
            pltpu.VMEM((n_slab, seq + (conv_w.shape[0] - 1) * SUBLANES, LANES), F32),
        ] + [pltpu.VMEM((2 * n_slab, seq, LANES), F32)] * (2 if final_only else 3),
        compiler_params=_params(2),
        name="lru_ctx" if final_only else "lru",
    )(*args)


def _ffn_up_kernel(x_ref, wg_ref, wv_ref, cwg_ref, cwv_ref, cbg_ref, cbv_ref, o_ref):
    x = x_ref[...]
    t_idx = lax.broadcasted_iota(jnp.int32, o_ref.shape, 0)

    def conv(u, cw, cb):
        return cw[0:1] * _shift_rows(u, 1, t_idx) + cw[1:2] * u + cw[2:3] * _shift_rows(u, -1, t_idx) + cb

    g = conv(jnp.dot(x, wg_ref[...].astype(BF16), preferred_element_type=F32), cwg_ref[...], cbg_ref[...])
    v = conv(jnp.dot(x, wv_ref[...].astype(BF16), preferred_element_type=F32), cwv_ref[...], cbv_ref[...])
    o_ref[...] = (_silu(g) * v).astype(o_ref.dtype)


def _ffn_up(h, w_up, conv_w, conv_b, *, batch, seq, bn):
    d = h.shape[1]
    f = w_up.shape[1] // 2
    nj = f // bn
    cb = conv_b.reshape(1, 2 * f)
    taps = conv_w.shape[0]
    return pl.pallas_call(
        _ffn_up_kernel,
        grid=(batch, nj),
        in_specs=[
            pl.BlockSpec((seq, d), lambda b, j: (b, 0)),
            pl.BlockSpec((d, bn), lambda b, j: (0, j)),
            pl.BlockSpec((d, bn), lambda b, j: (0, nj + j)),
            pl.BlockSpec((taps, bn), lambda b, j: (0, j)),
            pl.BlockSpec((taps, bn), lambda b, j: (0, nj + j)),
            pl.BlockSpec((1, bn), lambda b, j: (0, j)),
            pl.BlockSpec((1, bn), lambda b, j: (0, nj + j)),
        ],
        out_specs=pl.BlockSpec((seq, bn), lambda b, j: (b, j)),
        out_shape=jax.ShapeDtypeStruct((batch * seq, f), BF16),
        compiler_params=_params(2),
        name="ffn_up",
    )(h, w_up, w_up, conv_w, conv_w, cb, cb)


def _tile(n, pref):
    return pref if n % pref == 0 else n


def kernel(x, c, ctx, c_ctx, w_ada, b_ada, norm1, norm2, w_in, ret_decay_logit, lru_conv_w, lru_conv_b,
           lru_wa, lru_ba, lru_wx, lru_bx, lru_lambda, w_ret_o, w_lru_o, w_out, w_up, ffn_conv_w,
           ffn_conv_b, w_down, final_norm):
    batch, seq, d = x.shape
    ctx_len = ctx.shape[1]
    assert w_in.shape[0] == 1, "single-layer trunk"
    assert batch + 1 <= ADA_ROWS
    n_heads = ret_decay_logit.shape[-1]
    qk_w, v_w, lru_w = n_heads * RET_QK_DIM, n_heads * RET_V_DIM, lru_lambda.shape[-1]
    sizes = (("k", qk_w), ("v", v_w), ("x_lru", lru_w), ("q", qk_w), ("g_ret", v_w), ("g_lru", lru_w),
             ("m_ret", d), ("m_lru", d))
    col, off = {}, 0
    for name, size in sizes:
        col[name] = off
        off += size
    state_cols = col["q"]
    m = batch * seq
    f = w_down.shape[1]

    bf = lambda a: a.astype(BF16)
    w_down_b = bf(w_down[0])
    wa_b, wx_b = bf(0.5 * lru_wa[0]), bf(0.5 * lru_wx[0])

    c_rows = jnp.zeros((ADA_ROWS, d), F32).at[:batch].set(c).at[batch].set(c_ctx)
    mod = _ada(c_rows, w_ada[0], b_ada[0], tn=_tile(N_MOD * d, 512))
    sh1, sc1, g1, sh2, sc2, g2 = [mod[:batch, i * d:(i + 1) * d].reshape(batch, 1, d) for i in range(N_MOD)]
    csh, csc = [mod[batch, i * d:(i + 1) * d].reshape(1, 1, d) for i in range(2)]

    bm = _tile(seq, 1024)
    bn = 512

    h_ctx = _norm_mod(ctx, norm1[0], csh, csc, tm=_tile(ctx_len, 256))
    pc = _matmul(h_ctx, w_in[0], bm=_tile(batch * ctx_len, 1024), bn=_tile(state_cols, bn), n_cols=state_cols,
                 out_dtype=BF16, name="w_in_ctx")
    lru_args = (lru_conv_w[0], lru_conv_b[0], wa_b, wx_b, 0.5 * lru_ba[0], 0.5 * lru_bx[0], lru_lambda[0])
    h0 = _lru(pc, *lru_args, jnp.zeros((batch, 2, lru_w), F32), batch=batch, seq=ctx_len, col=col,
              final_only=True)

    h_lat = _norm_mod(x, norm1[0], sh1, sc1, tm=_tile(seq, 512))
    p = _matmul(h_lat, w_in[0], bm=_tile(seq, 2048), bn=_tile(off, bn), out_dtype=BF16, name="w_in")
    cos2, sin2 = _rope_tables(seq)
    a_ret = _retention(p, pc, cos2, sin2, ret_decay_logit[0], batch=batch, seq=seq, ctx_len=ctx_len,
                       n_heads=n_heads, col=col)
    a_lru = _lru(p, *lru_args, h0, batch=batch, seq=seq, col=col, final_only=False)

    bnd = _tile(d, bn)
    gate_spec = lambda name: pl.BlockSpec((bm, bnd), lambda i, j: (i, col[name] // bnd + j))
    tile_spec = pl.BlockSpec((bm, bnd), lambda i, j: (i, j))
    y1 = _matmul(a_ret, w_ret_o[0], bm=bm, bn=bnd, out_dtype=BF16, epilogue=_ep_gate, extra=(p,),
                 extra_specs=(gate_spec("m_ret"),), name="w_ret_o")
    y = _matmul(a_lru, w_lru_o[0], bm=bm, bn=bnd, out_dtype=BF16, epilogue=_ep_gate_add, extra=(p, y1),
                extra_specs=(gate_spec("m_lru"), tile_spec), name="w_lru_o")
    steps_per_batch = seq // bm
    vec_spec = pl.BlockSpec((None, 1, bnd), lambda i, j: (i // steps_per_batch, 0, j))
    x2d = x.reshape(m, d)
    x_lat = _matmul(y, w_out[0], bm=bm, bn=bnd, out_dtype=F32, epilogue=_ep_resid, extra=(x2d, g1),
                    extra_specs=(tile_spec, vec_spec), name="w_out")

    h2 = _norm_mod(x_lat.reshape(batch, seq, d), norm2[0], sh2, sc2, tm=_tile(seq, 512))
    act = _ffn_up(h2, w_up[0], ffn_conv_w[0], ffn_conv_b[0], batch=batch, seq=seq, bn=_tile(f, 256))
    out = _down_rms(act, w_down_b, x_lat, g2, final_norm, seq=seq, bm=_tile(seq, 512), bn=_tile(d, 256))
    return out.reshape(batch, seq, d)
```

```python
import functools
import math

import jax
import jax.numpy as jnp
from jax import lax
from jax.experimental import pallas as pl
from jax.experimental.pallas import tpu as pltpu

F32 = jnp.float32
BF16 = jnp.bfloat16

EPS = 1e-6
N_MOD = 6
RET_QK_DIM = 128
RET_V_DIM = 256
RET_CHUNK = 128
GRID_W = 64
ROPE_BASE = 10000.0
LRU_C = 8.0
SUBLANES = 8
LANES = 128
ADA_ROWS = SUBLANES
LRU_ROWS = 128
ROWS = 1024
ROWS_WIDE = 2048
ROWS_RESIDENT = 512
COLS = 512
COLS_NARROW = 256
NORM_ROWS = 512
ADA_COLS = 512

V7X_VMEM_BYTES = 64 * 1024 * 1024
VMEM_LIMIT_BYTES = V7X_VMEM_BYTES - 4 * 1024 * 1024


def _params(n_axes):
    return pltpu.CompilerParams(
        dimension_semantics=("arbitrary",) * n_axes,
        vmem_limit_bytes=VMEM_LIMIT_BYTES,
    )


def _sigmoid(x):
    return 0.5 * jnp.tanh(0.5 * x) + 0.5


def _silu(x):
    return x * _sigmoid(x)


def _gelu_tanh(x):
    k = math.sqrt(2.0 / math.pi)
    half = 0.5 * x
    return half * jnp.tanh(x * ((k * 0.044715) * (x * x) + k)) + half


def _softplus(x):
    return jnp.maximum(x, 0.0) + jnp.log1p(jnp.exp(-jnp.abs(x)))


def _log_sigmoid(x):
    return -_softplus(-x)


def _shift_rows(x, k, t_idx):
    n = x.shape[0]
    rolled = pltpu.roll(x, k % n, 0)
    if k > 0:
        return jnp.where(t_idx >= k, rolled, 0.0)
    return jnp.where(t_idx < n + k, rolled, 0.0)


def _ada_kernel(c_ref, w_ref, b_ref, o_ref):
    s = _silu(c_ref[...]).astype(BF16)
    o_ref[...] = jnp.dot(s, w_ref[...].astype(BF16), preferred_element_type=F32) + b_ref[...]


def _ada(c_rows, w, b, *, tn):
    d, n = w.shape
    return pl.pallas_call(
        _ada_kernel,
        grid=(n // tn,),
        in_specs=[
            pl.BlockSpec((ADA_ROWS, d), lambda j: (0, 0)),
            pl.BlockSpec((d, tn), lambda j: (0, j)),
            pl.BlockSpec((1, tn), lambda j: (0, j)),
        ],
        out_specs=pl.BlockSpec((ADA_ROWS, tn), lambda j: (0, j)),
        out_shape=jax.ShapeDtypeStruct((ADA_ROWS, n), F32),
        compiler_params=_params(1),
        name="ada",
    )(c_rows, w, b.reshape(1, n))


def _norm_mod_kernel(x_ref, g_ref, sh_ref, sc_ref, o_ref):
    x = x_ref[...]
    y = x * lax.rsqrt(jnp.mean(x * x, axis=-1, keepdims=True) + EPS)
    o_ref[...] = ((y * g_ref[...]) * (1.0 + sc_ref[...]) + sh_ref[...]).astype(o_ref.dtype)


def _norm_mod(x, gain, shift, scale, *, tm):
    b, t, d = x.shape
    per_batch = shift.shape[0] == b
    mod_idx = (lambda bi, i: (bi, 0, 0)) if per_batch else (lambda bi, i: (0, 0, 0))
    nt = t // tm
    return pl.pallas_call(
        _norm_mod_kernel,
        grid=(b, nt),
        in_specs=[
            pl.BlockSpec((None, tm, d), lambda bi, i: (bi, i, 0)),
            pl.BlockSpec((1, d), lambda bi, i: (0, 0)),
            pl.BlockSpec((None, 1, d), mod_idx),
            pl.BlockSpec((None, 1, d), mod_idx),
        ],
        out_specs=pl.BlockSpec((tm, d), lambda bi, i: (bi * nt + i, 0)),
        out_shape=jax.ShapeDtypeStruct((b * t, d), BF16),
        compiler_params=_params(2),
        name="norm_mod",
    )(x, gain.reshape(1, d), shift, scale)


def _down_kernel(x_ref, w_ref, r_ref, g_ref, gain_ref, o_ref, *, bn, nj):
    j = pl.program_id(1)
    acc = jnp.dot(x_ref[...], w_ref[...], preferred_element_type=F32)
    o_ref[:, pl.ds(pl.multiple_of(j * bn, bn), bn)] = r_ref[...] + g_ref[...] * acc

    @pl.when(j == nj - 1)
    def _():
        y = o_ref[...]
        o_ref[...] = y * lax.rsqrt(jnp.mean(y * y, axis=-1, keepdims=True) + EPS) * gain_ref[...]


def _down_rms(act, w, resid, gate, gain, *, seq, bm, bn):
    m, k = act.shape
    d = w.shape[1]
    nj = d // bn
    steps_per_batch = seq // bm
    return pl.pallas_call(
        functools.partial(_down_kernel, bn=bn, nj=nj),
        grid=(m // bm, nj),
        in_specs=[
            pl.BlockSpec((bm, k), lambda i, j: (i, 0)),
            pl.BlockSpec((k, bn), lambda i, j: (0, j)),
            pl.BlockSpec((bm, bn), lambda i, j: (i, j)),
            pl.BlockSpec((None, 1, bn), lambda i, j: (i // steps_per_batch, 0, j)),
            pl.BlockSpec((1, d), lambda i, j: (0, 0)),
        ],
        out_specs=pl.BlockSpec((bm, d), lambda i, j: (i, 0)),
        out_shape=jax.ShapeDtypeStruct((m, d), F32),
        compiler_params=_params(2),
        name="w_down_rms",
    )(act, w, resid, gate, gain.reshape(1, d))


def _mm_kernel(x_ref, w_ref, o_ref):
    o_ref[...] = jnp.dot(x_ref[...], w_ref[...].astype(BF16), preferred_element_type=F32).astype(o_ref.dtype)


def _matmul(x, w, *, bm, bn, n_cols=None, out_dtype, name):
    m, kdim = x.shape
    n = w.shape[1] if n_cols is None else n_cols
    assert m % bm == 0 and n % bn == 0
    return pl.pallas_call(
        _mm_kernel,
        grid=(m // bm, n // bn),
        in_specs=[pl.BlockSpec((bm, kdim), lambda i, j: (i, 0)), pl.BlockSpec((kdim, bn), lambda i, j: (0, j))],
        out_specs=pl.BlockSpec((bm, bn), lambda i, j: (i, j)),
        out_shape=jax.ShapeDtypeStruct((m, n), out_dtype),
        compiler_params=_params(2),
        name=name,
    )(x, w)


def _merge_kernel(a1_ref, w1_ref, m1_ref, a2_ref, w2_ref, m2_ref, o_ref):
    y1 = jnp.dot(a1_ref[...], w1_ref[...].astype(BF16), preferred_element_type=F32)
    y2 = jnp.dot(a2_ref[...], w2_ref[...].astype(BF16), preferred_element_type=F32)
    merged = _sigmoid(m1_ref[...].astype(F32)) * y1 + _sigmoid(m2_ref[...].astype(F32)) * y2
    o_ref[...] = merged.astype(o_ref.dtype)


def _merge(a1, w1, a2, w2, p, col1, col2, *, bm, bn):
    m, k = a1.shape
    d = w1.shape[1]
    lhs = pl.BlockSpec((bm, k), lambda i, j: (i, 0))
    rhs = pl.BlockSpec((k, bn), lambda i, j: (0, j))
    gate = lambda c: pl.BlockSpec((bm, bn), lambda i, j: (i, c // bn + j))
    return pl.pallas_call(
        _merge_kernel,
        grid=(m // bm, d // bn),
        in_specs=[lhs, rhs, gate(col1), lhs, rhs, gate(col2)],
        out_specs=pl.BlockSpec((bm, bn), lambda i, j: (i, j)),
        out_shape=jax.ShapeDtypeStruct((m, d), BF16),
        compiler_params=_params(2),
        name="merge",
    )(a1, w1, p, a2, w2, p)


def _out_kernel(y_ref, w_ref, x_ref, g_ref, gain_ref, sh_ref, sc_ref, xo_ref, ho_ref, *, bn, nj):
    j = pl.program_id(1)
    acc = jnp.dot(y_ref[...], w_ref[...].astype(BF16), preferred_element_type=F32)
    xo_ref[:, pl.ds(pl.multiple_of(j * bn, bn), bn)] = x_ref[...] + g_ref[...] * acc

    @pl.when(j == nj - 1)
    def _():
        x = xo_ref[...]
        y = x * lax.rsqrt(jnp.mean(x * x, axis=-1, keepdims=True) + EPS)
        ho_ref[...] = ((y * gain_ref[...]) * (1.0 + sc_ref[...]) + sh_ref[...]).astype(ho_ref.dtype)


def _out_norm(y, w, x, gate, gain, shift, scale, *, seq, bm, bn):
    m, k = y.shape
    d = w.shape[1]
    nj = d // bn
    steps_per_batch = seq // bm
    vec = lambda width, col: pl.BlockSpec((None, 1, width), lambda i, j: (i // steps_per_batch, 0, col(j)))
    return pl.pallas_call(
        functools.partial(_out_kernel, bn=bn, nj=nj),
        grid=(m // bm, nj),
        in_specs=[
            pl.BlockSpec((bm, k), lambda i, j: (i, 0)),
            pl.BlockSpec((k, bn), lambda i, j: (0, j)),
            pl.BlockSpec((bm, bn), lambda i, j: (i, j)),
            vec(bn, lambda j: j),
            pl.BlockSpec((1, d), lambda i, j: (0, 0)),
            vec(d, lambda j: 0),
            vec(d, lambda j: 0),
        ],
        out_specs=[pl.BlockSpec((bm, d), lambda i, j: (i, 0)), pl.BlockSpec((bm, d), lambda i, j: (i, 0))],
        out_shape=[jax.ShapeDtypeStruct((m, d), F32), jax.ShapeDtypeStruct((m, d), BF16)],
        compiler_params=_params(2),
        name="w_out_norm",
    )(y, w, x, gate, gain.reshape(1, d), shift, scale)


def _ret_kernel(q_ref, k_ref, v_ref, g_ref, kc_ref, vc_ref, cos_ref, sin_ref, dl_ref, o_ref,
                qs_ref, ks_ref, r_ref, *, seq, ctx_len):
    h = pl.program_id(1)
    n_heads = pl.num_programs(1)
    c = RET_CHUNK
    dk = RET_QK_DIM
    n_chunks = seq // c
    scale = dk ** -0.5
    tn_dims = (((0,), (0,)), ((), ()))
    nt_dims = (((1,), (1,)), ((), ()))

    lgf = _log_sigmoid(dl_ref[pl.ds(h, 1), :])
    lgb = _log_sigmoid(dl_ref[pl.ds(n_heads + h, 1), :])
    lgf_k = lgf[:, :dk]
    lgb_k = lgb[:, :dk]

    ri = lax.broadcasted_iota(jnp.int32, (c, c), 0).astype(F32)
    ci = lax.broadcasted_iota(jnp.int32, (c, c), 1).astype(F32)
    rel = ri - ci
    dmask = jnp.where(rel > 0, jnp.exp(jnp.maximum(rel, 0.0) * lgf_k),
                      jnp.where(rel < 0, jnp.exp(jnp.maximum(-rel, 0.0) * lgb_k), 2.0))
    qdf = jnp.exp((ri + 1.0) * lgf_k)
    qdb = jnp.exp((c - ri) * lgb_k)
    kdf = jnp.exp((c - 1.0 - ri) * lgf_k)
    kdb = jnp.exp(ri * lgb_k)
    cdf = jnp.exp(c * lgf)
    cdb = jnp.exp(c * lgb)

    li = lax.broadcasted_iota(jnp.int32, (ctx_len, dk), 0).astype(F32)
    kc = kc_ref[...].astype(F32) * scale
    vc = vc_ref[...]
    s0f = lax.dot_general((kc * jnp.exp((ctx_len - 1.0 - li) * lgf_k)).astype(BF16), vc, tn_dims,
                          preferred_element_type=F32)
    s0b = lax.dot_general((kc * jnp.exp(li * lgb_k)).astype(BF16), vc, tn_dims,
                          preferred_element_type=F32)

    cos = cos_ref[...]
    sin = sin_ref[...]
    q = q_ref[...].astype(F32)
    qs_ref[...] = (q * cos + pltpu.roll(q, dk // 2, 1) * sin).astype(BF16)
    k = k_ref[...].astype(F32)
    ks_ref[...] = ((k * cos + pltpu.roll(k, dk // 2, 1) * sin) * scale).astype(BF16)

    def rows(j):
        return pl.ds(pl.multiple_of(j * c, c), c)

    def bwd(i, r):
        j = n_chunks - 1 - i
        r_ref[j] = r.astype(BF16)
        kj = (ks_ref[rows(j), :].astype(F32) * kdb).astype(BF16)
        return cdb * r + lax.dot_general(kj, v_ref[rows(j), :], tn_dims, preferred_element_type=F32)

    lax.fori_loop(0, n_chunks, bwd, s0b, unroll=True)

    def fwd(j, s):
        qj = qs_ref[rows(j), :]
        kj = ks_ref[rows(j), :]
        vj = v_ref[rows(j), :]
        qk = lax.dot_general(qj, kj, nt_dims, preferred_element_type=F32)
        o = jnp.dot((qk * dmask).astype(BF16), vj, preferred_element_type=F32)
        qf = qj.astype(F32)
        qd = jnp.concatenate([(qf * qdf).astype(BF16), (qf * qdb).astype(BF16)], axis=1)
        st = jnp.concatenate([s.astype(BF16), r_ref[j]], axis=0)
        o = o + jnp.dot(qd, st, preferred_element_type=F32)
        mu = jnp.mean(o, axis=-1, keepdims=True)
        d = o - mu
        var = jnp.mean(d * d, axis=-1, keepdims=True)
        g = g_ref[rows(j), :].astype(F32)
        o_ref[rows(j), :] = (_silu(g) * (d * lax.rsqrt(var + EPS))).astype(o_ref.dtype)
        kd = (kj.astype(F32) * kdf).astype(BF16)
        return cdf * s + lax.dot_general(kd, vj, tn_dims, preferred_element_type=F32)

    lax.fori_loop(0, n_chunks, fwd, s0f, unroll=True)


def _retention(p, pc, cos2, sin2, decay_logit, *, batch, seq, ctx_len, n_heads, col):
    dk, dv = RET_QK_DIM, RET_V_DIM
    dl = jnp.broadcast_to(decay_logit.astype(F32).reshape(2 * n_heads, 1), (2 * n_heads, dv))
    kern = functools.partial(_ret_kernel, seq=seq, ctx_len=ctx_len)
    kb, vb, qb, gb = col["k"] // dk, col["v"] // dv, col["q"] // dk, col["g_ret"] // dv
    return pl.pallas_call(
        kern,
        grid=(batch, n_heads),
        in_specs=[
            pl.BlockSpec((seq, dk), lambda b, h: (b, qb + h)),
            pl.BlockSpec((seq, dk), lambda b, h: (b, kb + h)),
            pl.BlockSpec((seq, dv), lambda b, h: (b, vb + h)),
            pl.BlockSpec((seq, dv), lambda b, h: (b, gb + h)),
            pl.BlockSpec((ctx_len, dk), lambda b, h: (b, kb + h)),
            pl.BlockSpec((ctx_len, dv), lambda b, h: (b, vb + h)),
            pl.BlockSpec((seq, dk), lambda b, h: (0, 0)),
            pl.BlockSpec((seq, dk), lambda b, h: (0, 0)),
            pl.BlockSpec((2 * n_heads, dv), lambda b, h: (0, 0)),
        ],
        out_specs=pl.BlockSpec((seq, dv), lambda b, h: (b, h)),
        out_shape=jax.ShapeDtypeStruct((batch * seq, n_heads * dv), BF16),
        scratch_shapes=[
            pltpu.VMEM((seq, dk), BF16),
            pltpu.VMEM((seq, dk), BF16),
            pltpu.VMEM((seq // RET_CHUNK, dk, dv), BF16),
        ],
        compiler_params=_params(2),
        name="retention",
    )(p, p, p, p, pc, pc, cos2, sin2, dl)


def _rope_tables(seq):
    rows = seq // GRID_W
    row_ids = jnp.repeat(jnp.arange(rows, dtype=F32), GRID_W)
    col_ids = jnp.tile(jnp.arange(GRID_W, dtype=F32), rows)
    n_freq = RET_QK_DIM // 4
    inv_freq = ROPE_BASE ** (-jnp.arange(n_freq, dtype=F32) / n_freq)
    ang = jnp.concatenate([row_ids[:, None] * inv_freq, col_ids[:, None] * inv_freq], axis=-1)
    cos, sin = jnp.cos(ang), jnp.sin(ang)
    return jnp.concatenate([cos, cos], axis=-1), jnp.concatenate([-sin, sin], axis=-1)


def _lru_kernel(*refs, seq, final_only):
    if final_only:
        (x_ref, cw_ref, cb_ref, wa_ref, wx_ref, ba_ref, bx_ref, lam_ref, h0_ref, o_ref,
         xs_scr, xp_scr, a_scr, b_scr) = refs
        g_ref = y_scr = None
    else:
        (x_ref, cw_ref, cb_ref, wa_ref, wx_ref, ba_ref, bx_ref, lam_ref, h0_ref, g_ref, o_ref,
         xs_scr, xp_scr, a_scr, b_scr, y_scr) = refs
    blk = x_ref.shape[1]
    n_slab = blk // LANES
    seg = seq // SUBLANES
    pitch = seg + SUBLANES
    taps = cw_ref.shape[0]
    lead = taps - 2
    lanes = [slice(l * LANES, (l + 1) * LANES) for l in range(n_slab)]
    sub = lax.broadcasted_iota(jnp.int32, (SUBLANES, LANES), 0)

    for s in range(SUBLANES):
        xs = x_ref[s * seg:(s + 1) * seg, :].astype(F32)
        for l in range(n_slab):
            xs_scr[l, s * pitch:s * pitch + seg, :] = xs[:, lanes[l]]

    def tile(j):
        return pl.ds(pl.multiple_of(j * SUBLANES, SUBLANES), SUBLANES)

    def gather(j, carry):
        for l in range(n_slab):
            xp_scr[l, tile(j + lead), :] = xs_scr[l, pl.ds(j, SUBLANES, stride=pitch), :]
        return carry

    lax.fori_loop(0, seg, gather, 0, unroll=8)
    for l in range(n_slab):
        for k in range(lead):
            prev = xp_scr[l, (seg + k) * SUBLANES:(seg + k + 1) * SUBLANES, :]
            xp_scr[l, k * SUBLANES:(k + 1) * SUBLANES, :] = jnp.where(sub == 0, 0.0, pltpu.roll(prev, 1, 0))
        nxt = xp_scr[l, lead * SUBLANES:(lead + 1) * SUBLANES, :]
        xp_scr[l, (seg + lead) * SUBLANES:(seg + lead + 1) * SUBLANES, :] = jnp.where(
            sub == SUBLANES - 1, 0.0, pltpu.roll(nxt, SUBLANES - 1, 0))

    cw = cw_ref[...]
    cb = cb_ref[...]
    rate = [(0.25 * LRU_C) * _softplus(-lam_ref[z:z + 1, :]) for z in range(2)]

    def chunk(c, carry):
        r0 = pl.multiple_of(c * LRU_ROWS, LRU_ROWS)
        parts = []
        for l in range(n_slab):
            acc = cb[:, lanes[l]]
            for k in range(taps):
                acc = acc + cw[k:k + 1, lanes[l]] * xp_scr[l, pl.ds(r0 + k * SUBLANES, LRU_ROWS), :]
            parts.append(acc)
        xc = jnp.concatenate(parts, axis=1)
        xcb = xc.astype(BF16)
        for z in range(2):
            tr = jnp.tanh(jnp.dot(xcb, wa_ref[z], preferred_element_type=F32) + ba_ref[z:z + 1, :])
            ti = jnp.tanh(jnp.dot(xcb, wx_ref[z], preferred_element_type=F32) + bx_ref[z:z + 1, :])
            v = jnp.tanh(rate[z] * tr + rate[z])
            inv = 1.0 / (1.0 + v)
            a = (1.0 - v) * inv
            root = jnp.where(v > 0.0, v * lax.rsqrt(v), 0.0)
            b = (root * inv) * (xc * ti + xc)
            for l in range(n_slab):
                a_scr[z * n_slab + l, pl.ds(r0, LRU_ROWS), :] = a[:, lanes[l]]
                b_scr[z * n_slab + l, pl.ds(r0, LRU_ROWS), :] = b[:, lanes[l]]
        return carry

    lax.fori_loop(0, seq // LRU_ROWS, chunk, 0, unroll=8)

    def pass1(j, carry):
        jb = seg - 1 - j
        out = []
        for l in range(n_slab):
            hf, pf, hb, pb = carry[4 * l:4 * l + 4]
            af = a_scr[l, tile(j), :]
            ab = a_scr[n_slab + l, tile(jb), :]
            out += [af * hf + b_scr[l, tile(j), :], af * pf,
                    ab * hb + b_scr[n_slab + l, tile(jb), :], ab * pb]
        return tuple(out)

    zero = jnp.zeros((SUBLANES, LANES), F32)
    one = jnp.ones((SUBLANES, LANES), F32)
    ends = lax.fori_loop(0, seg, pass1, (zero, one, zero, one) * n_slab, unroll=8)

    h0 = h0_ref[...]
    starts = []
    finals = []
    for l in range(n_slab):
        hf, pf, hb, pb = ends[4 * l:4 * l + 4]
        h0f = jnp.broadcast_to(h0[0:1, lanes[l]], (SUBLANES, LANES))
        h0b = jnp.broadcast_to(h0[1:2, lanes[l]], (SUBLANES, LANES))
        cf, cbk = h0f, h0b
        for _ in range(SUBLANES - 1):
            cf = jnp.where(sub == 0, h0f, pltpu.roll(hf + pf * cf, 1, 0))
            cbk = jnp.where(sub == SUBLANES - 1, h0b, pltpu.roll(hb + pb * cbk, SUBLANES - 1, 0))
        starts += [cf, cbk]
        finals += [(hf + pf * cf)[SUBLANES - 1:SUBLANES, :], (hb + pb * cbk)[0:1, :]]

    if final_only:
        for l in range(n_slab):
            o_ref[0:1, lanes[l]] = finals[2 * l]
            o_ref[1:2, lanes[l]] = finals[2 * l + 1]
        return

    def pass2(j, carry):
        jb = seg - 1 - j
        out = []
        for l in range(n_slab):
            hf, hb = carry[2 * l:2 * l + 2]
            hf = a_scr[l, tile(j), :] * hf + b_scr[l, tile(j), :]
            y_scr[l, tile(j), :] = hf
            hb = a_scr[n_slab + l, tile(jb), :] * hb + b_scr[n_slab + l, tile(jb), :]
            y_scr[n_slab + l, tile(jb), :] = hb
            out += [hf, hb]
        return tuple(out)

    lax.fori_loop(0, seg, pass2, tuple(starts), unroll=8)

    for s in range(SUBLANES):
        for l in range(n_slab):
            y = (y_scr[l, pl.ds(s, seg, stride=SUBLANES), :] + y_scr[n_slab + l, pl.ds(s, seg, stride=SUBLANES), :])
            g = g_ref[s * seg:(s + 1) * seg, lanes[l]].astype(F32)
            o_ref[s * seg:(s + 1) * seg, lanes[l]] = (_gelu_tanh(g) * y).astype(o_ref.dtype)


def _lru(p, conv_w, conv_b, wa, wx, ba, bx, lam, h0, *, batch, seq, col, final_only):
    nb, blk = wa.shape[1], wa.shape[2]
    w = nb * blk
    n_slab = blk // LANES
    assert seq % (2 * SUBLANES * SUBLANES) == 0 and seq % LRU_ROWS == 0
    xb = col["x_lru"] // blk
    in_specs = [
        pl.BlockSpec((seq, blk), lambda b, n: (b, xb + n)),
        pl.BlockSpec((conv_w.shape[0], blk), lambda b, n: (0, n)),
        pl.BlockSpec((1, blk), lambda b, n: (0, n)),
        pl.BlockSpec((2, None, blk, blk), lambda b, n: (0, n, 0, 0)),
        pl.BlockSpec((2, None, blk, blk), lambda b, n: (0, n, 0, 0)),
        pl.BlockSpec((2, blk), lambda b, n: (0, n)),
        pl.BlockSpec((2, blk), lambda b, n: (0, n)),
        pl.BlockSpec((2, blk), lambda b, n: (0, n)),
        pl.BlockSpec((None, 2, blk), lambda b, n: (b, 0, n)),
    ]
    args = [p, conv_w, conv_b.reshape(1, w), wa, wx, ba, bx, lam, h0]
    if final_only:
        out_spec = pl.BlockSpec((None, 2, blk), lambda b, n: (b, 0, n))
        out_shape = jax.ShapeDtypeStruct((batch, 2, w), F32)
    else:
        gb = col["g_lru"] // blk
        in_specs.append(pl.BlockSpec((seq, blk), lambda b, n: (b, gb + n)))
        args.append(p)
        out_spec = pl.BlockSpec((seq, blk), lambda b, n: (b, n))
        out_shape = jax.ShapeDtypeStruct((batch * seq, w), BF16)
    return pl.pallas_call(
        functools.partial(_lru_kernel, seq=seq, final_only=final_only),
        grid=(batch, nb),
        in_specs=in_specs,
        out_specs=out_spec,
        out_shape=out_shape,
        scratch_shapes=[
            pltpu.VMEM((n_slab, seq + SUBLANES * SUBLANES, LANES), F32),
            pltpu.VMEM((n_slab, seq + (conv_w.shape[0] - 1) * SUBLANES, LANES), F32),
        ] + [pltpu.VMEM((2 * n_slab, seq, LANES), F32)] * (2 if final_only else 3),
        compiler_params=_params(2),
        name="lru_ctx" if final_only else "lru",
    )(*args)


def _ffn_up_kernel(x_ref, wg_ref, wv_ref, cwg_ref, cwv_ref, cbg_ref, cbv_ref, o_ref):
    x = x_ref[...]
    t_idx = lax.broadcasted_iota(jnp.int32, o_ref.shape, 0)

    def conv(u, cw, cb):
        return cw[0:1] * _shift_rows(u, 1, t_idx) + cw[1:2] * u + cw[2:3] * _shift_rows(u, -1, t_idx) + cb

    g = conv(jnp.dot(x, wg_ref[...].astype(BF16), preferred_element_type=F32), cwg_ref[...], cbg_ref[...])
    v = conv(jnp.dot(x, wv_ref[...].astype(BF16), preferred_element_type=F32), cwv_ref[...], cbv_ref[...])
    o_ref[...] = (_silu(g) * v).astype(o_ref.dtype)


def _ffn_up(h, w_up, conv_w, conv_b, *, batch, seq, bn):
    d = h.shape[1]
    f = w_up.shape[1] // 2
    nj = f // bn
    cb = conv_b.reshape(1, 2 * f)
    taps = conv_w.shape[0]
    return pl.pallas_call(
        _ffn_up_kernel,
        grid=(batch, nj),
        in_specs=[
            pl.BlockSpec((seq, d), lambda b, j: (b, 0)),
            pl.BlockSpec((d, bn), lambda b, j: (0, j)),
            pl.BlockSpec((d, bn), lambda b, j: (0, nj + j)),
            pl.BlockSpec((taps, bn), lambda b, j: (0, j)),
            pl.BlockSpec((taps, bn), lambda b, j: (0, nj + j)),
            pl.BlockSpec((1, bn), lambda b, j: (0, j)),
            pl.BlockSpec((1, bn), lambda b, j: (0, nj + j)),
        ],
        out_specs=pl.BlockSpec((seq, bn), lambda b, j: (b, j)),
        out_shape=jax.ShapeDtypeStruct((batch * seq, f), BF16),
        compiler_params=_params(2),
        name="ffn_up",
    )(h, w_up, w_up, conv_w, conv_w, cb, cb)


def _tile(n, pref):
    return pref if n % pref == 0 else n


def kernel(x, c, ctx, c_ctx, w_ada, b_ada, norm1, norm2, w_in, ret_decay_logit, lru_conv_w, lru_conv_b,
           lru_wa, lru_ba, lru_wx, lru_bx, lru_lambda, w_ret_o, w_lru_o, w_out, w_up, ffn_conv_w,
           ffn_conv_b, w_down, final_norm):
    batch, seq, d = x.shape
    ctx_len = ctx.shape[1]
    assert w_in.shape[0] == 1, "single-layer trunk"
    assert batch + 1 <= ADA_ROWS
    n_heads = ret_decay_logit.shape[-1]
    qk_w, v_w, lru_w = n_heads * RET_QK_DIM, n_heads * RET_V_DIM, lru_lambda.shape[-1]
    sizes = (("k", qk_w), ("v", v_w), ("x_lru", lru_w), ("q", qk_w), ("g_ret", v_w), ("g_lru", lru_w),
             ("m_ret", d), ("m_lru", d))
    col, off = {}, 0
    for name, size in sizes:
        col[name] = off
        off += size
    state_cols = col["q"]
    m = batch * seq
    f = w_down.shape[1]

    bf = lambda a: a.astype(BF16)
    w_down_b = bf(w_down[0])
    wa_b, wx_b = bf(0.5 * lru_wa[0]), bf(0.5 * lru_wx[0])

    c_rows = jnp.zeros((ADA_ROWS, d), F32).at[:batch].set(c).at[batch].set(c_ctx)
    mod = _ada(c_rows, w_ada[0], b_ada[0], tn=_tile(N_MOD * d, ADA_COLS))
    sh1, sc1, g1, sh2, sc2, g2 = [mod[:batch, i * d:(i + 1) * d].reshape(batch, 1, d) for i in range(N_MOD)]
    csh, csc = [mod[batch, i * d:(i + 1) * d].reshape(1, 1, d) for i in range(2)]

    h_ctx = _norm_mod(ctx, norm1[0], csh, csc, tm=_tile(ctx_len, NORM_ROWS))
    pc = _matmul(h_ctx, w_in[0], bm=_tile(batch * ctx_len, ROWS), bn=_tile(state_cols, COLS), n_cols=state_cols,
                 out_dtype=BF16, name="w_in_ctx")
    lru_args = (lru_conv_w[0], lru_conv_b[0], wa_b, wx_b, 0.5 * lru_ba[0], 0.5 * lru_bx[0], lru_lambda[0])
    h0 = _lru(pc, *lru_args, jnp.zeros((batch, 2, lru_w), F32), batch=batch, seq=ctx_len, col=col,
              final_only=True)

    h_lat = _norm_mod(x, norm1[0], sh1, sc1, tm=_tile(seq, NORM_ROWS))
    p = _matmul(h_lat, w_in[0], bm=_tile(seq, ROWS_WIDE), bn=_tile(off, COLS), out_dtype=BF16, name="w_in")
    cos2, sin2 = _rope_tables(seq)
    a_ret = _retention(p, pc, cos2, sin2, ret_decay_logit[0], batch=batch, seq=seq, ctx_len=ctx_len,
                       n_heads=n_heads, col=col)
    a_lru = _lru(p, *lru_args, h0, batch=batch, seq=seq, col=col, final_only=False)
    y = _merge(a_ret, w_ret_o[0], a_lru, w_lru_o[0], p, col["m_ret"], col["m_lru"], bm=_tile(seq, ROWS),
               bn=_tile(d, COLS_NARROW))
    x_lat, h2 = _out_norm(y, w_out[0], x.reshape(m, d), g1, norm2[0], sh2, sc2, seq=seq,
                          bm=_tile(seq, ROWS_RESIDENT), bn=_tile(d, COLS))

    act = _ffn_up(h2, w_up[0], ffn_conv_w[0], ffn_conv_b[0], batch=batch, seq=seq, bn=_tile(f, COLS_NARROW))
    out = _down_rms(act, w_down_b, x_lat, g2, final_norm, seq=seq, bm=_tile(seq, ROWS_RESIDENT),
                    bn=_tile(d, COLS_NARROW))
    return out.reshape(batch, seq, d)
```

```python
import functools
import math

import jax
import jax.numpy as jnp
from jax import lax
from jax.experimental import pallas as pl
from jax.experimental.pallas import tpu as pltpu

F32 = jnp.float32
BF16 = jnp.bfloat16

EPS = 1e-6
N_MOD = 6
RET_QK_DIM = 128
RET_V_DIM = 256
RET_CHUNK = 128
GRID_W = 64
ROPE_BASE = 10000.0
LRU_C = 8.0
SUBLANES = 8
LANES = 128
ADA_ROWS = SUBLANES
LRU_ROWS = 128
LRU_CHANNELS = 512
ROWS = 1024
ROWS_WIDE = 2048
ROWS_RESIDENT = 512
COLS = 512
COLS_NARROW = 256
NORM_ROWS = 512
ADA_COLS = 512

V7X_VMEM_BYTES = 64 * 1024 * 1024
VMEM_LIMIT_BYTES = V7X_VMEM_BYTES - 4 * 1024 * 1024


def _params(n_axes):
    return pltpu.CompilerParams(
        dimension_semantics=("arbitrary",) * n_axes,
        vmem_limit_bytes=VMEM_LIMIT_BYTES,
    )


def _sigmoid(x):
    return 0.5 * jnp.tanh(0.5 * x) + 0.5


def _silu(x):
    return x * _sigmoid(x)


def _gelu_tanh(x):
    k = math.sqrt(2.0 / math.pi)
    half = 0.5 * x
    return half * jnp.tanh(x * ((k * 0.044715) * (x * x) + k)) + half


def _softplus(x):
    return jnp.maximum(x, 0.0) + jnp.log1p(jnp.exp(-jnp.abs(x)))


def _log_sigmoid(x):
    return -_softplus(-x)


def _shift_rows(x, k, t_idx):
    n = x.shape[0]
    rolled = pltpu.roll(x, k % n, 0)
    if k > 0:
        return jnp.where(t_idx >= k, rolled, 0.0)
    return jnp.where(t_idx < n + k, rolled, 0.0)


def _ada_kernel(c_ref, w_ref, b_ref, o_ref):
    s = _silu(c_ref[...]).astype(BF16)
    o_ref[...] = jnp.dot(s, w_ref[...].astype(BF16), preferred_element_type=F32) + b_ref[...]


def _ada(c_rows, w, b, *, tn):
    d, n = w.shape
    return pl.pallas_call(
        _ada_kernel,
        grid=(n // tn,),
        in_specs=[
            pl.BlockSpec((ADA_ROWS, d), lambda j: (0, 0)),
            pl.BlockSpec((d, tn), lambda j: (0, j)),
            pl.BlockSpec((1, tn), lambda j: (0, j)),
        ],
        out_specs=pl.BlockSpec((ADA_ROWS, tn), lambda j: (0, j)),
        out_shape=jax.ShapeDtypeStruct((ADA_ROWS, n), F32),
        compiler_params=_params(1),
        name="ada",
    )(c_rows, w, b.reshape(1, n))


def _norm_mod_kernel(x_ref, g_ref, sh_ref, sc_ref, o_ref):
    x = x_ref[...]
    y = x * lax.rsqrt(jnp.mean(x * x, axis=-1, keepdims=True) + EPS)
    o_ref[...] = ((y * g_ref[...]) * (1.0 + sc_ref[...]) + sh_ref[...]).astype(o_ref.dtype)


def _norm_mod(x, gain, shift, scale, *, tm):
    b, t, d = x.shape
    per_batch = shift.shape[0] == b
    mod_idx = (lambda bi, i: (bi, 0, 0)) if per_batch else (lambda bi, i: (0, 0, 0))
    nt = t // tm
    return pl.pallas_call(
        _norm_mod_kernel,
        grid=(b, nt),
        in_specs=[
            pl.BlockSpec((None, tm, d), lambda bi, i: (bi, i, 0)),
            pl.BlockSpec((1, d), lambda bi, i: (0, 0)),
            pl.BlockSpec((None, 1, d), mod_idx),
            pl.BlockSpec((None, 1, d), mod_idx),
        ],
        out_specs=pl.BlockSpec((tm, d), lambda bi, i: (bi * nt + i, 0)),
        out_shape=jax.ShapeDtypeStruct((b * t, d), BF16),
        compiler_params=_params(2),
        name="norm_mod",
    )(x, gain.reshape(1, d), shift, scale)


def _down_kernel(x_ref, w_ref, r_ref, g_ref, gain_ref, o_ref, *, bn, nj):
    j = pl.program_id(1)
    acc = jnp.dot(x_ref[...], w_ref[...], preferred_element_type=F32)
    o_ref[:, pl.ds(pl.multiple_of(j * bn, bn), bn)] = r_ref[...] + g_ref[...] * acc

    @pl.when(j == nj - 1)
    def _():
        y = o_ref[...]
        o_ref[...] = y * lax.rsqrt(jnp.mean(y * y, axis=-1, keepdims=True) + EPS) * gain_ref[...]


def _down_rms(act, w, resid, gate, gain, *, seq, bm, bn):
    m, k = act.shape
    d = w.shape[1]
    nj = d // bn
    steps_per_batch = seq // bm
    return pl.pallas_call(
        functools.partial(_down_kernel, bn=bn, nj=nj),
        grid=(m // bm, nj),
        in_specs=[
            pl.BlockSpec((bm, k), lambda i, j: (i, 0)),
            pl.BlockSpec((k, bn), lambda i, j: (0, j)),
            pl.BlockSpec((bm, bn), lambda i, j: (i, j)),
            pl.BlockSpec((None, 1, bn), lambda i, j: (i // steps_per_batch, 0, j)),
            pl.BlockSpec((1, d), lambda i, j: (0, 0)),
        ],
        out_specs=pl.BlockSpec((bm, d), lambda i, j: (i, 0)),
        out_shape=jax.ShapeDtypeStruct((m, d), F32),
        compiler_params=_params(2),
        name="w_down_rms",
    )(act, w, resid, gate, gain.reshape(1, d))


def _mm_kernel(x_ref, w_ref, *rest, epilogue):
    *extra, o_ref = rest
    acc = jnp.dot(x_ref[...], w_ref[...].astype(BF16), preferred_element_type=F32)
    o_ref[...] = epilogue(acc, *extra).astype(o_ref.dtype)


def _matmul(x, w, *, bm, bn, n_cols=None, out_dtype, epilogue=None, extra=(), extra_specs=(), name):
    m, kdim = x.shape
    n = w.shape[1] if n_cols is None else n_cols
    assert m % bm == 0 and n % bn == 0
    if epilogue is None:
        epilogue = lambda acc: acc
    return pl.pallas_call(
        functools.partial(_mm_kernel, epilogue=epilogue),
        grid=(m // bm, n // bn),
        in_specs=[
            pl.BlockSpec((bm, kdim), lambda i, j: (i, 0)),
            pl.BlockSpec((kdim, bn), lambda i, j: (0, j)),
            *extra_specs,
        ],
        out_specs=pl.BlockSpec((bm, bn), lambda i, j: (i, j)),
        out_shape=jax.ShapeDtypeStruct((m, n), out_dtype),
        compiler_params=_params(2),
        name=name,
    )(x, w, *extra)


def _merge_kernel(a1_ref, w1_ref, m1_ref, a2_ref, w2_ref, m2_ref, o_ref):
    y1 = jnp.dot(a1_ref[...], w1_ref[...].astype(BF16), preferred_element_type=F32)
    y2 = jnp.dot(a2_ref[...], w2_ref[...].astype(BF16), preferred_element_type=F32)
    merged = _sigmoid(m1_ref[...].astype(F32)) * y1 + _sigmoid(m2_ref[...].astype(F32)) * y2
    o_ref[...] = merged.astype(o_ref.dtype)


def _merge(a1, w1, a2, w2, p, col1, col2, *, bm, bn):
    m, k = a1.shape
    d = w1.shape[1]
    lhs = pl.BlockSpec((bm, k), lambda i, j: (i, 0))
    rhs = pl.BlockSpec((k, bn), lambda i, j: (0, j))
    gate = lambda c: pl.BlockSpec((bm, bn), lambda i, j: (i, c // bn + j))
    return pl.pallas_call(
        _merge_kernel,
        grid=(m // bm, d // bn),
        in_specs=[lhs, rhs, gate(col1), lhs, rhs, gate(col2)],
        out_specs=pl.BlockSpec((bm, bn), lambda i, j: (i, j)),
        out_shape=jax.ShapeDtypeStruct((m, d), BF16),
        compiler_params=_params(2),
        name="merge",
    )(a1, w1, p, a2, w2, p)


def _ep_resid(acc, x_ref, g_ref):
    return x_ref[...] + g_ref[...] * acc


def _ret_kernel(q_ref, k_ref, v_ref, g_ref, kc_ref, vc_ref, cos_ref, sin_ref, dl_ref, o_ref,
                qs_ref, ks_ref, r_ref, *, seq, ctx_len):
    h = pl.program_id(1)
    n_heads = pl.num_programs(1)
    c = RET_CHUNK
    dk = RET_QK_DIM
    n_chunks = seq // c
    scale = dk ** -0.5
    tn_dims = (((0,), (0,)), ((), ()))
    nt_dims = (((1,), (1,)), ((), ()))

    lgf = _log_sigmoid(dl_ref[pl.ds(h, 1), :])
    lgb = _log_sigmoid(dl_ref[pl.ds(n_heads + h, 1), :])
    lgf_k = lgf[:, :dk]
    lgb_k = lgb[:, :dk]

    ri = lax.broadcasted_iota(jnp.int32, (c, c), 0).astype(F32)
    ci = lax.broadcasted_iota(jnp.int32, (c, c), 1).astype(F32)
    rel = ri - ci
    dmask = jnp.where(rel > 0, jnp.exp(jnp.maximum(rel, 0.0) * lgf_k),
                      jnp.where(rel < 0, jnp.exp(jnp.maximum(-rel, 0.0) * lgb_k), 2.0))
    qdf = jnp.exp((ri + 1.0) * lgf_k)
    qdb = jnp.exp((c - ri) * lgb_k)
    kdf = jnp.exp((c - 1.0 - ri) * lgf_k)
    kdb = jnp.exp(ri * lgb_k)
    cdf = jnp.exp(c * lgf)
    cdb = jnp.exp(c * lgb)

    li = lax.broadcasted_iota(jnp.int32, (ctx_len, dk), 0).astype(F32)
    kc = kc_ref[...].astype(F32) * scale
    vc = vc_ref[...]
    s0f = lax.dot_general((kc * jnp.exp((ctx_len - 1.0 - li) * lgf_k)).astype(BF16), vc, tn_dims,
                          preferred_element_type=F32)
    s0b = lax.dot_general((kc * jnp.exp(li * lgb_k)).astype(BF16), vc, tn_dims,
                          preferred_element_type=F32)

    cos = cos_ref[...]
    sin = sin_ref[...]
    q = q_ref[...].astype(F32)
    qs_ref[...] = (q * cos + pltpu.roll(q, dk // 2, 1) * sin).astype(BF16)
    k = k_ref[...].astype(F32)
    ks_ref[...] = ((k * cos + pltpu.roll(k, dk // 2, 1) * sin) * scale).astype(BF16)

    def rows(j):
        return pl.ds(pl.multiple_of(j * c, c), c)

    def bwd(i, r):
        j = n_chunks - 1 - i
        r_ref[j] = r.astype(BF16)
        kj = (ks_ref[rows(j), :].astype(F32) * kdb).astype(BF16)
        return cdb * r + lax.dot_general(kj, v_ref[rows(j), :], tn_dims, preferred_element_type=F32)

    lax.fori_loop(0, n_chunks, bwd, s0b, unroll=True)

    def fwd(j, s):
        qj = qs_ref[rows(j), :]
        kj = ks_ref[rows(j), :]
        vj = v_ref[rows(j), :]
        qk = lax.dot_general(qj, kj, nt_dims, preferred_element_type=F32)
        o = jnp.dot((qk * dmask).astype(BF16), vj, preferred_element_type=F32)
        qf = qj.astype(F32)
        qd = jnp.concatenate([(qf * qdf).astype(BF16), (qf * qdb).astype(BF16)], axis=1)
        st = jnp.concatenate([s.astype(BF16), r_ref[j]], axis=0)
        o = o + jnp.dot(qd, st, preferred_element_type=F32)
        mu = jnp.mean(o, axis=-1, keepdims=True)
        d = o - mu
        var = jnp.mean(d * d, axis=-1, keepdims=True)
        g = g_ref[rows(j), :].astype(F32)
        o_ref[rows(j), :] = (_silu(g) * (d * lax.rsqrt(var + EPS))).astype(o_ref.dtype)
        kd = (kj.astype(F32) * kdf).astype(BF16)
        return cdf * s + lax.dot_general(kd, vj, tn_dims, preferred_element_type=F32)

    lax.fori_loop(0, n_chunks, fwd, s0f, unroll=True)


def _retention(p, pc, cos2, sin2, decay_logit, *, batch, seq, ctx_len, n_heads, col):
    dk, dv = RET_QK_DIM, RET_V_DIM
    dl = jnp.broadcast_to(decay_logit.astype(F32).reshape(2 * n_heads, 1), (2 * n_heads, dv))
    kern = functools.partial(_ret_kernel, seq=seq, ctx_len=ctx_len)
    kb, vb, qb, gb = col["k"] // dk, col["v"] // dv, col["q"] // dk, col["g_ret"] // dv
    return pl.pallas_call(
        kern,
        grid=(batch, n_heads),
        in_specs=[
            pl.BlockSpec((seq, dk), lambda b, h: (b, qb + h)),
            pl.BlockSpec((seq, dk), lambda b, h: (b, kb + h)),
            pl.BlockSpec((seq, dv), lambda b, h: (b, vb + h)),
            pl.BlockSpec((seq, dv), lambda b, h: (b, gb + h)),
            pl.BlockSpec((ctx_len, dk), lambda b, h: (b, kb + h)),
            pl.BlockSpec((ctx_len, dv), lambda b, h: (b, vb + h)),
            pl.BlockSpec((seq, dk), lambda b, h: (0, 0)),
            pl.BlockSpec((seq, dk), lambda b, h: (0, 0)),
            pl.BlockSpec((2 * n_heads, dv), lambda b, h: (0, 0)),
        ],
        out_specs=pl.BlockSpec((seq, dv), lambda b, h: (b, h)),
        out_shape=jax.ShapeDtypeStruct((batch * seq, n_heads * dv), BF16),
        scratch_shapes=[
            pltpu.VMEM((seq, dk), BF16),
            pltpu.VMEM((seq, dk), BF16),
            pltpu.VMEM((seq // RET_CHUNK, dk, dv), BF16),
        ],
        compiler_params=_params(2),
        name="retention",
    )(p, p, p, p, pc, pc, cos2, sin2, dl)


def _rope_tables(seq):
    rows = seq // GRID_W
    row_ids = jnp.repeat(jnp.arange(rows, dtype=F32), GRID_W)
    col_ids = jnp.tile(jnp.arange(GRID_W, dtype=F32), rows)
    n_freq = RET_QK_DIM // 4
    inv_freq = ROPE_BASE ** (-jnp.arange(n_freq, dtype=F32) / n_freq)
    ang = jnp.concatenate([row_ids[:, None] * inv_freq, col_ids[:, None] * inv_freq], axis=-1)
    cos, sin = jnp.cos(ang), jnp.sin(ang)
    return jnp.concatenate([cos, cos], axis=-1), jnp.concatenate([-sin, sin], axis=-1)


def _lru_kernel(*refs, seq, final_only):
    if final_only:
        (x_ref, cw_ref, cb_ref, wa_ref, wx_ref, ba_ref, bx_ref, lam_ref, h0_ref, o_ref,
         xs_scr, xp_scr, a_scr, b_scr) = refs
        g_ref = y_scr = None
    else:
        (x_ref, cw_ref, cb_ref, wa_ref, wx_ref, ba_ref, bx_ref, lam_ref, h0_ref, g_ref, o_ref,
         xs_scr, xp_scr, a_scr, b_scr, y_scr) = refs
    blk = x_ref.shape[1]
    n_slab = blk // LANES
    seg = seq // SUBLANES
    pitch = seg + SUBLANES
    taps = cw_ref.shape[0]
    lead = taps - 2
    lanes = [slice(l * LANES, (l + 1) * LANES) for l in range(n_slab)]
    sub = lax.broadcasted_iota(jnp.int32, (SUBLANES, LANES), 0)

    for s in range(SUBLANES):
        xs = x_ref[s * seg:(s + 1) * seg, :].astype(F32)
        for l in range(n_slab):
            xs_scr[l, s * pitch:s * pitch + seg, :] = xs[:, lanes[l]]

    def tile(j):
        return pl.ds(pl.multiple_of(j * SUBLANES, SUBLANES), SUBLANES)

    def gather(j, carry):
        for l in range(n_slab):
            xp_scr[l, tile(j + lead), :] = xs_scr[l, pl.ds(j, SUBLANES, stride=pitch), :]
        return carry

    lax.fori_loop(0, seg, gather, 0, unroll=8)
    for l in range(n_slab):
        for k in range(lead):
            prev = xp_scr[l, (seg + k) * SUBLANES:(seg + k + 1) * SUBLANES, :]
            xp_scr[l, k * SUBLANES:(k + 1) * SUBLANES, :] = jnp.where(sub == 0, 0.0, pltpu.roll(prev, 1, 0))
        nxt = xp_scr[l, lead * SUBLANES:(lead + 1) * SUBLANES, :]
        xp_scr[l, (seg + lead) * SUBLANES:(seg + lead + 1) * SUBLANES, :] = jnp.where(
            sub == SUBLANES - 1, 0.0, pltpu.roll(nxt, SUBLANES - 1, 0))

    cw = cw_ref[...]
    cb = cb_ref[...]
    rate = [(0.25 * LRU_C) * _softplus(-lam_ref[z:z + 1, :]) for z in range(2)]

    def chunk(c, carry):
        r0 = pl.multiple_of(c * LRU_ROWS, LRU_ROWS)
        parts = []
        for l in range(n_slab):
            acc = cb[:, lanes[l]]
            for k in range(taps):
                acc = acc + cw[k:k + 1, lanes[l]] * xp_scr[l, pl.ds(r0 + k * SUBLANES, LRU_ROWS), :]
            parts.append(acc)
        gate = wa_ref.shape[-1]
        for g in range(blk // gate):
            cols = slice(g * gate, (g + 1) * gate)
            xc = jnp.concatenate(parts[g * gate // LANES:(g + 1) * gate // LANES], axis=1)
            xcb = xc.astype(BF16)
            for z in range(2):
                tr = jnp.tanh(jnp.dot(xcb, wa_ref[z, g], preferred_element_type=F32) + ba_ref[z:z + 1, cols])
                ti = jnp.tanh(jnp.dot(xcb, wx_ref[z, g], preferred_element_type=F32) + bx_ref[z:z + 1, cols])
                v = jnp.tanh(rate[z][:, cols] * tr + rate[z][:, cols])
                inv = 1.0 / (1.0 + v)
                a = (1.0 - v) * inv
                root = jnp.where(v > 0.0, v * lax.rsqrt(v), 0.0)
                b = (root * inv) * (xc * ti + xc)
                for l in range(gate // LANES):
                    slab = z * n_slab + g * gate // LANES + l
                    a_scr[slab, pl.ds(r0, LRU_ROWS), :] = a[:, l * LANES:(l + 1) * LANES]
                    b_scr[slab, pl.ds(r0, LRU_ROWS), :] = b[:, l * LANES:(l + 1) * LANES]
        return carry

    lax.fori_loop(0, seq // LRU_ROWS, chunk, 0, unroll=4)

    def pass1(j, carry):
        jb = seg - 1 - j
        out = []
        for l in range(n_slab):
            hf, pf, hb, pb = carry[4 * l:4 * l + 4]
            af = a_scr[l, tile(j), :]
            ab = a_scr[n_slab + l, tile(jb), :]
            out += [af * hf + b_scr[l, tile(j), :], af * pf,
                    ab * hb + b_scr[n_slab + l, tile(jb), :], ab * pb]
        return tuple(out)

    zero = jnp.zeros((SUBLANES, LANES), F32)
    one = jnp.ones((SUBLANES, LANES), F32)
    ends = lax.fori_loop(0, seg, pass1, (zero, one, zero, one) * n_slab, unroll=8)

    h0 = h0_ref[...]
    starts = []
    finals = []
    for l in range(n_slab):
        hf, pf, hb, pb = ends[4 * l:4 * l + 4]
        h0f = jnp.broadcast_to(h0[0:1, lanes[l]], (SUBLANES, LANES))
        h0b = jnp.broadcast_to(h0[1:2, lanes[l]], (SUBLANES, LANES))
        cf, cbk = h0f, h0b
        for _ in range(SUBLANES - 1):
            cf = jnp.where(sub == 0, h0f, pltpu.roll(hf + pf * cf, 1, 0))
            cbk = jnp.where(sub == SUBLANES - 1, h0b, pltpu.roll(hb + pb * cbk, SUBLANES - 1, 0))
        starts += [cf, cbk]
        finals += [(hf + pf * cf)[SUBLANES - 1:SUBLANES, :], (hb + pb * cbk)[0:1, :]]

    if final_only:
        for l in range(n_slab):
            o_ref[0:1, lanes[l]] = finals[2 * l]
            o_ref[1:2, lanes[l]] = finals[2 * l + 1]
        return

    def pass2(j, carry):
        jb = seg - 1 - j
        out = []
        for l in range(n_slab):
            hf, hb = carry[2 * l:2 * l + 2]
            hf = a_scr[l, tile(j), :] * hf + b_scr[l, tile(j), :]
            y_scr[l, tile(j), :] = hf
            hb = a_scr[n_slab + l, tile(jb), :] * hb + b_scr[n_slab + l, tile(jb), :]
            y_scr[n_slab + l, tile(jb), :] = hb
            out += [hf, hb]
        return tuple(out)

    lax.fori_loop(0, seg, pass2, tuple(starts), unroll=8)

    for s in range(SUBLANES):
        for l in range(n_slab):
            y = (y_scr[l, pl.ds(s, seg, stride=SUBLANES), :] + y_scr[n_slab + l, pl.ds(s, seg, stride=SUBLANES), :])
            g = g_ref[s * seg:(s + 1) * seg, lanes[l]].astype(F32)
            o_ref[s * seg:(s + 1) * seg, lanes[l]] = (_gelu_tanh(g) * y).astype(o_ref.dtype)


def _lru(p, conv_w, conv_b, wa, wx, ba, bx, lam, h0, *, batch, seq, col, final_only):
    n_gate, gate = wa.shape[1], wa.shape[2]
    w = n_gate * gate
    starts = [col["x_lru"]] + ([] if final_only else [col["g_lru"]])
    wide = LRU_CHANNELS % gate == 0 and all(c % LRU_CHANNELS == 0 for c in starts + [w])
    blk = LRU_CHANNELS if wide else gate
    nb = w // blk
    n_slab = blk // LANES
    assert seq % (2 * SUBLANES * SUBLANES) == 0 and seq % LRU_ROWS == 0
    xb = col["x_lru"] // blk
    in_specs = [
        pl.BlockSpec((seq, blk), lambda b, n: (b, xb + n)),
        pl.BlockSpec((conv_w.shape[0], blk), lambda b, n: (0, n)),
        pl.BlockSpec((1, blk), lambda b, n: (0, n)),
        pl.BlockSpec((2, blk // gate, gate, gate), lambda b, n: (0, n, 0, 0)),
        pl.BlockSpec((2, blk // gate, gate, gate), lambda b, n: (0, n, 0, 0)),
        pl.BlockSpec((2, blk), lambda b, n: (0, n)),
        pl.BlockSpec((2, blk), lambda b, n: (0, n)),
        pl.BlockSpec((2, blk), lambda b, n: (0, n)),
        pl.BlockSpec((None, 2, blk), lambda b, n: (b, 0, n)),
    ]
    args = [p, conv_w, conv_b.reshape(1, w), wa, wx, ba, bx, lam, h0]
    if final_only:
        out_spec = pl.BlockSpec((None, 2, blk), lambda b, n: (b, 0, n))
        out_shape = jax.ShapeDtypeStruct((batch, 2, w), F32)
    else:
        gb = col["g_lru"] // blk
        in_specs.append(pl.BlockSpec((seq, blk), lambda b, n: (b, gb + n)))
        args.append(p)
        out_spec = pl.BlockSpec((seq, blk), lambda b, n: (b, n))
        out_shape = jax.ShapeDtypeStruct((batch * seq, w), BF16)
    return pl.pallas_call(
        functools.partial(_lru_kernel, seq=seq, final_only=final_only),
        grid=(batch, nb),
        in_specs=in_specs,
        out_specs=out_spec,
        out_shape=out_shape,
        scratch_shapes=[
            pltpu.VMEM((n_slab, seq + SUBLANES * SUBLANES, LANES), F32),
            pltpu.VMEM((n_slab, seq + (conv_w.shape[0] - 1) * SUBLANES, LANES), F32),
        ] + [pltpu.VMEM((2 * n_slab, seq, LANES), F32)] * (2 if final_only else 3),
        compiler_params=_params(2),
        name="lru_ctx" if final_only else "lru",
    )(*args)


def _ffn_up_kernel(x_ref, wg_ref, wv_ref, cwg_ref, cwv_ref, cbg_ref, cbv_ref, o_ref):
    x = x_ref[...]
    t_idx = lax.broadcasted_iota(jnp.int32, o_ref.shape, 0)

    def conv(u, cw, cb):
        return cw[0:1] * _shift_rows(u, 1, t_idx) + cw[1:2] * u + cw[2:3] * _shift_rows(u, -1, t_idx) + cb

    g = conv(jnp.dot(x, wg_ref[...].astype(BF16), preferred_element_type=F32), cwg_ref[...], cbg_ref[...])
    v = conv(jnp.dot(x, wv_ref[...].astype(BF16), preferred_element_type=F32), cwv_ref[...], cbv_ref[...])
    o_ref[...] = (_silu(g) * v).astype(o_ref.dtype)


def _ffn_up(h, w_up, conv_w, conv_b, *, batch, seq, bn):
    d = h.shape[1]
    f = w_up.shape[1] // 2
    nj = f // bn
    cb = conv_b.reshape(1, 2 * f)
    taps = conv_w.shape[0]
    return pl.pallas_call(
        _ffn_up_kernel,
        grid=(batch, nj),
        in_specs=[
            pl.BlockSpec((seq, d), lambda b, j: (b, 0)),
            pl.BlockSpec((d, bn), lambda b, j: (0, j)),
            pl.BlockSpec((d, bn), lambda b, j: (0, nj + j)),
            pl.BlockSpec((taps, bn), lambda b, j: (0, j)),
            pl.BlockSpec((taps, bn), lambda b, j: (0, nj + j)),
            pl.BlockSpec((1, bn), lambda b, j: (0, j)),
            pl.BlockSpec((1, bn), lambda b, j: (0, nj + j)),
        ],
        out_specs=pl.BlockSpec((seq, bn), lambda b, j: (b, j)),
        out_shape=jax.ShapeDtypeStruct((batch * seq, f), BF16),
        compiler_params=_params(2),
        name="ffn_up",
    )(h, w_up, w_up, conv_w, conv_w, cb, cb)


def _tile(n, pref):
    return pref if n % pref == 0 else n


def kernel(x, c, ctx, c_ctx, w_ada, b_ada, norm1, norm2, w_in, ret_decay_logit, lru_conv_w, lru_conv_b,
           lru_wa, lru_ba, lru_wx, lru_bx, lru_lambda, w_ret_o, w_lru_o, w_out, w_up, ffn_conv_w,
           ffn_conv_b, w_down, final_norm):
    batch, seq, d = x.shape
    ctx_len = ctx.shape[1]
    assert w_in.shape[0] == 1, "single-layer trunk"
    assert batch + 1 <= ADA_ROWS
    n_heads = ret_decay_logit.shape[-1]
    qk_w, v_w, lru_w = n_heads * RET_QK_DIM, n_heads * RET_V_DIM, lru_lambda.shape[-1]
    sizes = (("k", qk_w), ("v", v_w), ("x_lru", lru_w), ("q", qk_w), ("g_ret", v_w), ("g_lru", lru_w),
             ("m_ret", d), ("m_lru", d))
    col, off = {}, 0
    for name, size in sizes:
        col[name] = off
        off += size
    state_cols = col["q"]
    m = batch * seq
    f = w_down.shape[1]

    bf = lambda a: a.astype(BF16)
    w_down_b = bf(w_down[0])
    wa_b, wx_b = bf(0.5 * lru_wa[0]), bf(0.5 * lru_wx[0])

    c_rows = jnp.zeros((ADA_ROWS, d), F32).at[:batch].set(c).at[batch].set(c_ctx)
    mod = _ada(c_rows, w_ada[0], b_ada[0], tn=_tile(N_MOD * d, ADA_COLS))
    sh1, sc1, g1, sh2, sc2, g2 = [mod[:batch, i * d:(i + 1) * d].reshape(batch, 1, d) for i in range(N_MOD)]
    csh, csc = [mod[batch, i * d:(i + 1) * d].reshape(1, 1, d) for i in range(2)]

    h_ctx = _norm_mod(ctx, norm1[0], csh, csc, tm=_tile(ctx_len, NORM_ROWS))
    pc = _matmul(h_ctx, w_in[0], bm=_tile(batch * ctx_len, ROWS), bn=_tile(state_cols, COLS), n_cols=state_cols,
                 out_dtype=BF16, name="w_in_ctx")
    lru_args = (lru_conv_w[0], lru_conv_b[0], wa_b, wx_b, 0.5 * lru_ba[0], 0.5 * lru_bx[0], lru_lambda[0])
    h0 = _lru(pc, *lru_args, jnp.zeros((batch, 2, lru_w), F32), batch=batch, seq=ctx_len, col=col,
              final_only=True)

    h_lat = _norm_mod(x, norm1[0], sh1, sc1, tm=_tile(seq, NORM_ROWS))
    p = _matmul(h_lat, w_in[0], bm=_tile(seq, ROWS_WIDE), bn=_tile(off, COLS), out_dtype=BF16, name="w_in")
    cos2, sin2 = _rope_tables(seq)
    a_ret = _retention(p, pc, cos2, sin2, ret_decay_logit[0], batch=batch, seq=seq, ctx_len=ctx_len,
                       n_heads=n_heads, col=col)
    a_lru = _lru(p, *lru_args, h0, batch=batch, seq=seq, col=col, final_only=False)
    bm, bnd = _tile(seq, ROWS), _tile(d, COLS)
    y = _merge(a_ret, w_ret_o[0], a_lru, w_lru_o[0], p, col["m_ret"], col["m_lru"], bm=bm,
               bn=_tile(d, COLS_NARROW))
    steps_per_batch = seq // bm
    x_lat = _matmul(y, w_out[0], bm=bm, bn=bnd, out_dtype=F32, epilogue=_ep_resid, extra=(x.reshape(m, d), g1),
                    extra_specs=(pl.BlockSpec((bm, bnd), lambda i, j: (i, j)),
                                 pl.BlockSpec((None, 1, bnd), lambda i, j: (i // steps_per_batch, 0, j))),
                    name="w_out")

    h2 = _norm_mod(x_lat.reshape(batch, seq, d), norm2[0], sh2, sc2, tm=_tile(seq, NORM_ROWS))
    act = _ffn_up(h2, w_up[0], ffn_conv_w[0], ffn_conv_b[0], batch=batch, seq=seq, bn=_tile(f, COLS_NARROW))
    out = _down_rms(act, w_down_b, x_lat, g2, final_norm, seq=seq, bm=_tile(seq, ROWS_RESIDENT),
                    bn=_tile(d, COLS_NARROW))
    return out.reshape(batch, seq, d)
```

```python
import functools
import math

import jax
import jax.numpy as jnp
from jax import lax
from jax.experimental import pallas as pl
from jax.experimental.pallas import tpu as pltpu

F32 = jnp.float32
BF16 = jnp.bfloat16

EPS = 1e-6
N_MOD = 6
RET_QK_DIM = 128
RET_V_DIM = 256
RET_CHUNK = 128
GRID_W = 64
ROPE_BASE = 10000.0
LRU_C = 8.0
SUBLANES = 8
LANES = 128
ADA_ROWS = SUBLANES
LRU_ROWS = 128
LRU_CHANNELS = 512
ROWS = 1024
ROWS_WIDE = 2048
ROWS_RESIDENT = 512
COLS = 512
COLS_NARROW = 256
NORM_ROWS = 512
ADA_COLS = 512

V7X_VMEM_BYTES = 64 * 1024 * 1024
VMEM_LIMIT_BYTES = V7X_VMEM_BYTES - 4 * 1024 * 1024


def _params(n_axes):
    return pltpu.CompilerParams(
        dimension_semantics=("arbitrary",) * n_axes,
        vmem_limit_bytes=VMEM_LIMIT_BYTES,
    )


def _sigmoid(x):
    return 0.5 * jnp.tanh(0.5 * x) + 0.5


def _silu(x):
    return x * _sigmoid(x)


def _gelu_tanh(x):
    k = math.sqrt(2.0 / math.pi)
    half = 0.5 * x
    return half * jnp.tanh(x * ((k * 0.044715) * (x * x) + k)) + half


def _softplus(x):
    return jnp.maximum(x, 0.0) + jnp.log1p(jnp.exp(-jnp.abs(x)))


def _log_sigmoid(x):
    return -_softplus(-x)


def _shift_rows(x, k, t_idx):
    n = x.shape[0]
    rolled = pltpu.roll(x, k % n, 0)
    if k > 0:
        return jnp.where(t_idx >= k, rolled, 0.0)
    return jnp.where(t_idx < n + k, rolled, 0.0)


def _ada_kernel(c_ref, w_ref, b_ref, o_ref):
    s = _silu(c_ref[...]).astype(BF16)
    o_ref[...] = jnp.dot(s, w_ref[...].astype(BF16), preferred_element_type=F32) + b_ref[...]


def _ada(c_rows, w, b, *, tn):
    d, n = w.shape
    return pl.pallas_call(
        _ada_kernel,
        grid=(n // tn,),
        in_specs=[
            pl.BlockSpec((ADA_ROWS, d), lambda j: (0, 0)),
            pl.BlockSpec((d, tn), lambda j: (0, j)),
            pl.BlockSpec((1, tn), lambda j: (0, j)),
        ],
        out_specs=pl.BlockSpec((ADA_ROWS, tn), lambda j: (0, j)),
        out_shape=jax.ShapeDtypeStruct((ADA_ROWS, n), F32),
        compiler_params=_params(1),
        name="ada",
    )(c_rows, w, b.reshape(1, n))


def _norm_mod_kernel(x_ref, g_ref, sh_ref, sc_ref, o_ref):
    x = x_ref[...]
    y = x * lax.rsqrt(jnp.mean(x * x, axis=-1, keepdims=True) + EPS)
    o_ref[...] = ((y * g_ref[...]) * (1.0 + sc_ref[...]) + sh_ref[...]).astype(o_ref.dtype)


def _norm_mod(x, gain, shift, scale, *, tm):
    b, t, d = x.shape
    per_batch = shift.shape[0] == b
    mod_idx = (lambda bi, i: (bi, 0, 0)) if per_batch else (lambda bi, i: (0, 0, 0))
    nt = t // tm
    return pl.pallas_call(
        _norm_mod_kernel,
        grid=(b, nt),
        in_specs=[
            pl.BlockSpec((None, tm, d), lambda bi, i: (bi, i, 0)),
            pl.BlockSpec((1, d), lambda bi, i: (0, 0)),
            pl.BlockSpec((None, 1, d), mod_idx),
            pl.BlockSpec((None, 1, d), mod_idx),
        ],
        out_specs=pl.BlockSpec((tm, d), lambda bi, i: (bi * nt + i, 0)),
        out_shape=jax.ShapeDtypeStruct((b * t, d), BF16),
        compiler_params=_params(2),
        name="norm_mod",
    )(x, gain.reshape(1, d), shift, scale)


def _down_kernel(x_ref, w_ref, r_ref, g_ref, gain_ref, o_ref, *, bn, nj):
    j = pl.program_id(1)
    acc = jnp.dot(x_ref[...], w_ref[...], preferred_element_type=F32)
    o_ref[:, pl.ds(pl.multiple_of(j * bn, bn), bn)] = r_ref[...] + g_ref[...] * acc

    @pl.when(j == nj - 1)
    def _():
        y = o_ref[...]
        o_ref[...] = y * lax.rsqrt(jnp.mean(y * y, axis=-1, keepdims=True) + EPS) * gain_ref[...]


def _down_rms(act, w, resid, gate, gain, *, seq, bm, bn):
    m, k = act.shape
    d = w.shape[1]
    nj = d // bn
    steps_per_batch = seq // bm
    return pl.pallas_call(
        functools.partial(_down_kernel, bn=bn, nj=nj),
        grid=(m // bm, nj),
        in_specs=[
            pl.BlockSpec((bm, k), lambda i, j: (i, 0)),
            pl.BlockSpec((k, bn), lambda i, j: (0, j)),
            pl.BlockSpec((bm, bn), lambda i, j: (i, j)),
            pl.BlockSpec((None, 1, bn), lambda i, j: (i // steps_per_batch, 0, j)),
            pl.BlockSpec((1, d), lambda i, j: (0, 0)),
        ],
        out_specs=pl.BlockSpec((bm, d), lambda i, j: (i, 0)),
        out_shape=jax.ShapeDtypeStruct((m, d), F32),
        compiler_params=_params(2),
        name="w_down_rms",
    )(act, w, resid, gate, gain.reshape(1, d))


def _mm_kernel(x_ref, w_ref, *rest, epilogue):
    *extra, o_ref = rest
    acc = jnp.dot(x_ref[...], w_ref[...].astype(BF16), preferred_element_type=F32)
    o_ref[...] = epilogue(acc, *extra).astype(o_ref.dtype)


def _matmul(x, w, *, bm, bn, n_cols=None, out_dtype, epilogue=None, extra=(), extra_specs=(), name):
    m, kdim = x.shape
    n = w.shape[1] if n_cols is None else n_cols
    assert m % bm == 0 and n % bn == 0
    if epilogue is None:
        epilogue = lambda acc: acc
    return pl.pallas_call(
        functools.partial(_mm_kernel, epilogue=epilogue),
        grid=(m // bm, n // bn),
        in_specs=[
            pl.BlockSpec((bm, kdim), lambda i, j: (i, 0)),
            pl.BlockSpec((kdim, bn), lambda i, j: (0, j)),
            *extra_specs,
        ],
        out_specs=pl.BlockSpec((bm, bn), lambda i, j: (i, j)),
        out_shape=jax.ShapeDtypeStruct((m, n), out_dtype),
        compiler_params=_params(2),
        name=name,
    )(x, w, *extra)


def _merge_kernel(a1_ref, w1_ref, m1_ref, a2_ref, w2_ref, m2_ref, o_ref):
    y1 = jnp.dot(a1_ref[...], w1_ref[...].astype(BF16), preferred_element_type=F32)
    y2 = jnp.dot(a2_ref[...], w2_ref[...].astype(BF16), preferred_element_type=F32)
    merged = _sigmoid(m1_ref[...].astype(F32)) * y1 + _sigmoid(m2_ref[...].astype(F32)) * y2
    o_ref[...] = merged.astype(o_ref.dtype)


def _merge(a1, w1, a2, w2, p, col1, col2, *, bm, bn):
    m, k = a1.shape
    d = w1.shape[1]
    lhs = pl.BlockSpec((bm, k), lambda i, j: (i, 0))
    rhs = pl.BlockSpec((k, bn), lambda i, j: (0, j))
    gate = lambda c: pl.BlockSpec((bm, bn), lambda i, j: (i, c // bn + j))
    return pl.pallas_call(
        _merge_kernel,
        grid=(m // bm, d // bn),
        in_specs=[lhs, rhs, gate(col1), lhs, rhs, gate(col2)],
        out_specs=pl.BlockSpec((bm, bn), lambda i, j: (i, j)),
        out_shape=jax.ShapeDtypeStruct((m, d), BF16),
        compiler_params=_params(2),
        name="merge",
    )(a1, w1, p, a2, w2, p)


def _ep_resid(acc, x_ref, g_ref):
    return x_ref[...] + g_ref[...] * acc


W_RING = 3


def _out_ring_kernel(y_ref, w_hbm, x_ref, g_ref, o_ref, wbuf, sem, *, bn, nj, n_steps):
    s = pl.program_id(0) * nj + pl.program_id(1)

    def copy(step):
        col0 = pl.multiple_of((step % nj) * bn, bn)
        slot = step % W_RING
        return pltpu.make_async_copy(w_hbm.at[:, pl.ds(col0, bn)], wbuf.at[slot], sem.at[slot])

    @pl.when(s == 0)
    def _():
        copy(0).start()
        copy(1).start()

    @pl.when(s + 2 < n_steps)
    def _():
        copy(s + 2).start()

    copy(s).wait()
    acc = jnp.dot(y_ref[...], wbuf[s % W_RING].astype(BF16), preferred_element_type=F32)
    o_ref[...] = x_ref[...] + g_ref[...] * acc


def _out_proj(y, w, x, gate, *, seq, bm, bn):
    m, k = y.shape
    d = w.shape[1]
    nj = d // bn
    n_steps = (m // bm) * nj
    assert n_steps >= 2
    steps_per_batch = seq // bm
    return pl.pallas_call(
        functools.partial(_out_ring_kernel, bn=bn, nj=nj, n_steps=n_steps),
        grid=(m // bm, nj),
        in_specs=[
            pl.BlockSpec((bm, k), lambda i, j: (i, 0)),
            pl.BlockSpec(memory_space=pl.ANY),
            pl.BlockSpec((bm, bn), lambda i, j: (i, j)),
            pl.BlockSpec((None, 1, bn), lambda i, j: (i // steps_per_batch, 0, j)),
        ],
        out_specs=pl.BlockSpec((bm, bn), lambda i, j: (i, j)),
        out_shape=jax.ShapeDtypeStruct((m, d), F32),
        scratch_shapes=[pltpu.VMEM((W_RING, k, bn), F32), pltpu.SemaphoreType.DMA((W_RING,))],
        compiler_params=_params(2),
        name="w_out",
    )(y, w, x, gate)


def _ret_kernel(q_ref, k_ref, v_ref, g_ref, kc_ref, vc_ref, cos_ref, sin_ref, dl_ref, o_ref,
                qs_ref, ks_ref, r_ref, *, seq, ctx_len):
    h = pl.program_id(1)
    n_heads = pl.num_programs(1)
    c = RET_CHUNK
    dk = RET_QK_DIM
    n_chunks = seq // c
    scale = dk ** -0.5
    tn_dims = (((0,), (0,)), ((), ()))
    nt_dims = (((1,), (1,)), ((), ()))

    lgf = _log_sigmoid(dl_ref[pl.ds(h, 1), :])
    lgb = _log_sigmoid(dl_ref[pl.ds(n_heads + h, 1), :])
    lgf_k = lgf[:, :dk]
    lgb_k = lgb[:, :dk]

    ri = lax.broadcasted_iota(jnp.int32, (c, c), 0).astype(F32)
    ci = lax.broadcasted_iota(jnp.int32, (c, c), 1).astype(F32)
    rel = ri - ci
    dmask = jnp.where(rel > 0, jnp.exp(jnp.maximum(rel, 0.0) * lgf_k),
                      jnp.where(rel < 0, jnp.exp(jnp.maximum(-rel, 0.0) * lgb_k), 2.0))
    qdf = jnp.exp((ri + 1.0) * lgf_k)
    qdb = jnp.exp((c - ri) * lgb_k)
    kdf = jnp.exp((c - 1.0 - ri) * lgf_k)
    kdb = jnp.exp(ri * lgb_k)
    cdf = jnp.exp(c * lgf)
    cdb = jnp.exp(c * lgb)

    li = lax.broadcasted_iota(jnp.int32, (ctx_len, dk), 0).astype(F32)
    kc = kc_ref[...].astype(F32) * scale
    vc = vc_ref[...]
    s0f = lax.dot_general((kc * jnp.exp((ctx_len - 1.0 - li) * lgf_k)).astype(BF16), vc, tn_dims,
                          preferred_element_type=F32)
    s0b = lax.dot_general((kc * jnp.exp(li * lgb_k)).astype(BF16), vc, tn_dims,
                          preferred_element_type=F32)

    cos = cos_ref[...]
    sin = sin_ref[...]
    q = q_ref[...].astype(F32)
    qs_ref[...] = (q * cos + pltpu.roll(q, dk // 2, 1) * sin).astype(BF16)
    k = k_ref[...].astype(F32)
    ks_ref[...] = ((k * cos + pltpu.roll(k, dk // 2, 1) * sin) * scale).astype(BF16)

    def rows(j):
        return pl.ds(pl.multiple_of(j * c, c), c)

    def bwd(i, r):
        j = n_chunks - 1 - i
        r_ref[j] = r.astype(BF16)
        kj = (ks_ref[rows(j), :].astype(F32) * kdb).astype(BF16)
        return cdb * r + lax.dot_general(kj, v_ref[rows(j), :], tn_dims, preferred_element_type=F32)

    lax.fori_loop(0, n_chunks, bwd, s0b, unroll=True)

    def fwd(j, s):
        qj = qs_ref[rows(j), :]
        kj = ks_ref[rows(j), :]
        vj = v_ref[rows(j), :]
        qk = lax.dot_general(qj, kj, nt_dims, preferred_element_type=F32)
        o = jnp.dot((qk * dmask).astype(BF16), vj, preferred_element_type=F32)
        qf = qj.astype(F32)
        qd = jnp.concatenate([(qf * qdf).astype(BF16), (qf * qdb).astype(BF16)], axis=1)
        st = jnp.concatenate([s.astype(BF16), r_ref[j]], axis=0)
        o = o + jnp.dot(qd, st, preferred_element_type=F32)
        mu = jnp.mean(o, axis=-1, keepdims=True)
        d = o - mu
        var = jnp.mean(d * d, axis=-1, keepdims=True)
        g = g_ref[rows(j), :].astype(F32)
        o_ref[rows(j), :] = (_silu(g) * (d * lax.rsqrt(var + EPS))).astype(o_ref.dtype)
        kd = (kj.astype(F32) * kdf).astype(BF16)
        return cdf * s + lax.dot_general(kd, vj, tn_dims, preferred_element_type=F32)

    lax.fori_loop(0, n_chunks, fwd, s0f, unroll=True)


def _retention(p, pc, cos2, sin2, decay_logit, *, batch, seq, ctx_len, n_heads, col):
    dk, dv = RET_QK_DIM, RET_V_DIM
    dl = jnp.broadcast_to(decay_logit.astype(F32).reshape(2 * n_heads, 1), (2 * n_heads, dv))
    kern = functools.partial(_ret_kernel, seq=seq, ctx_len=ctx_len)
    kb, vb, qb, gb = col["k"] // dk, col["v"] // dv, col["q"] // dk, col["g_ret"] // dv
    return pl.pallas_call(
        kern,
        grid=(batch, n_heads),
        in_specs=[
            pl.BlockSpec((seq, dk), lambda b, h: (b, qb + h)),
            pl.BlockSpec((seq, dk), lambda b, h: (b, kb + h)),
            pl.BlockSpec((seq, dv), lambda b, h: (b, vb + h)),
            pl.BlockSpec((seq, dv), lambda b, h: (b, gb + h)),
            pl.BlockSpec((ctx_len, dk), lambda b, h: (b, kb + h)),
            pl.BlockSpec((ctx_len, dv), lambda b, h: (b, vb + h)),
            pl.BlockSpec((seq, dk), lambda b, h: (0, 0)),
            pl.BlockSpec((seq, dk), lambda b, h: (0, 0)),
            pl.BlockSpec((2 * n_heads, dv), lambda b, h: (0, 0)),
        ],
        out_specs=pl.BlockSpec((seq, dv), lambda b, h: (b, h)),
        out_shape=jax.ShapeDtypeStruct((batch * seq, n_heads * dv), BF16),
        scratch_shapes=[
            pltpu.VMEM((seq, dk), BF16),
            pltpu.VMEM((seq, dk), BF16),
            pltpu.VMEM((seq // RET_CHUNK, dk, dv), BF16),
        ],
        compiler_params=_params(2),
        name="retention",
    )(p, p, p, p, pc, pc, cos2, sin2, dl)


def _rope_tables(seq):
    rows = seq // GRID_W
    row_ids = jnp.repeat(jnp.arange(rows, dtype=F32), GRID_W)
    col_ids = jnp.tile(jnp.arange(GRID_W, dtype=F32), rows)
    n_freq = RET_QK_DIM // 4
    inv_freq = ROPE_BASE ** (-jnp.arange(n_freq, dtype=F32) / n_freq)
    ang = jnp.concatenate([row_ids[:, None] * inv_freq, col_ids[:, None] * inv_freq], axis=-1)
    cos, sin = jnp.cos(ang), jnp.sin(ang)
    return jnp.concatenate([cos, cos], axis=-1), jnp.concatenate([-sin, sin], axis=-1)


def _lru_kernel(*refs, seq, final_only):
    if final_only:
        (x_ref, cw_ref, cb_ref, wa_ref, wx_ref, ba_ref, bx_ref, lam_ref, h0_ref, o_ref,
         xs_scr, xp_scr, a_scr, b_scr) = refs
        g_ref = y_scr = None
    else:
        (x_ref, cw_ref, cb_ref, wa_ref, wx_ref, ba_ref, bx_ref, lam_ref, h0_ref, g_ref, o_ref,
         xs_scr, xp_scr, a_scr, b_scr, y_scr) = refs
    blk = x_ref.shape[1]
    n_slab = blk // LANES
    seg = seq // SUBLANES
    pitch = seg + SUBLANES
    taps = cw_ref.shape[0]
    lead = taps - 2
    lanes = [slice(l * LANES, (l + 1) * LANES) for l in range(n_slab)]
    sub = lax.broadcasted_iota(jnp.int32, (SUBLANES, LANES), 0)

    for s in range(SUBLANES):
        xs = x_ref[s * seg:(s + 1) * seg, :].astype(F32)
        for l in range(n_slab):
            xs_scr[l, s * pitch:s * pitch + seg, :] = xs[:, lanes[l]]

    def tile(j):
        return pl.ds(pl.multiple_of(j * SUBLANES, SUBLANES), SUBLANES)

    def gather(j, carry):
        for l in range(n_slab):
            xp_scr[l, tile(j + lead), :] = xs_scr[l, pl.ds(j, SUBLANES, stride=pitch), :]
        return carry

    lax.fori_loop(0, seg, gather, 0, unroll=8)
    for l in range(n_slab):
        for k in range(lead):
            prev = xp_scr[l, (seg + k) * SUBLANES:(seg + k + 1) * SUBLANES, :]
            xp_scr[l, k * SUBLANES:(k + 1) * SUBLANES, :] = jnp.where(sub == 0, 0.0, pltpu.roll(prev, 1, 0))
        nxt = xp_scr[l, lead * SUBLANES:(lead + 1) * SUBLANES, :]
        xp_scr[l, (seg + lead) * SUBLANES:(seg + lead + 1) * SUBLANES, :] = jnp.where(
            sub == SUBLANES - 1, 0.0, pltpu.roll(nxt, SUBLANES - 1, 0))

    cw = cw_ref[...]
    cb = cb_ref[...]
    rate = [(0.25 * LRU_C) * _softplus(-lam_ref[z:z + 1, :]) for z in range(2)]

    def chunk(c, carry):
        r0 = pl.multiple_of(c * LRU_ROWS, LRU_ROWS)
        parts = []
        for l in range(n_slab):
            acc = cb[:, lanes[l]]
            for k in range(taps):
                acc = acc + cw[k:k + 1, lanes[l]] * xp_scr[l, pl.ds(r0 + k * SUBLANES, LRU_ROWS), :]
            parts.append(acc)
        gate = wa_ref.shape[-1]
        for g in range(blk // gate):
            cols = slice(g * gate, (g + 1) * gate)
            xc = jnp.concatenate(parts[g * gate // LANES:(g + 1) * gate // LANES], axis=1)
            xcb = xc.astype(BF16)
            for z in range(2):
                tr = jnp.tanh(jnp.dot(xcb, wa_ref[z, g], preferred_element_type=F32) + ba_ref[z:z + 1, cols])
                ti = jnp.tanh(jnp.dot(xcb, wx_ref[z, g], preferred_element_type=F32) + bx_ref[z:z + 1, cols])
                v = jnp.tanh(rate[z][:, cols] * tr + rate[z][:, cols])
                inv = 1.0 / (1.0 + v)
                a = (1.0 - v) * inv
                root = jnp.where(v > 0.0, v * lax.rsqrt(v), 0.0)
                b = (root * inv) * (xc * ti + xc)
                for l in range(gate // LANES):
                    slab = z * n_slab + g * gate // LANES + l
                    a_scr[slab, pl.ds(r0, LRU_ROWS), :] = a[:, l * LANES:(l + 1) * LANES]
                    b_scr[slab, pl.ds(r0, LRU_ROWS), :] = b[:, l * LANES:(l + 1) * LANES]
        return carry

    lax.fori_loop(0, seq // LRU_ROWS, chunk, 0, unroll=4)

    def pass1(j, carry):
        jb = seg - 1 - j
        out = []
        for l in range(n_slab):
            hf, pf, hb, pb = carry[4 * l:4 * l + 4]
            af = a_scr[l, tile(j), :]
            ab = a_scr[n_slab + l, tile(jb), :]
            out += [af * hf + b_scr[l, tile(j), :], af * pf,
                    ab * hb + b_scr[n_slab + l, tile(jb), :], ab * pb]
        return tuple(out)

    zero = jnp.zeros((SUBLANES, LANES), F32)
    one = jnp.ones((SUBLANES, LANES), F32)
    ends = lax.fori_loop(0, seg, pass1, (zero, one, zero, one) * n_slab, unroll=8)

    h0 = h0_ref[...]
    starts = []
    finals = []
    for l in range(n_slab):
        hf, pf, hb, pb = ends[4 * l:4 * l + 4]
        h0f = jnp.broadcast_to(h0[0:1, lanes[l]], (SUBLANES, LANES))
        h0b = jnp.broadcast_to(h0[1:2, lanes[l]], (SUBLANES, LANES))
        cf, cbk = h0f, h0b
        for _ in range(SUBLANES - 1):
            cf = jnp.where(sub == 0, h0f, pltpu.roll(hf + pf * cf, 1, 0))
            cbk = jnp.where(sub == SUBLANES - 1, h0b, pltpu.roll(hb + pb * cbk, SUBLANES - 1, 0))
        starts += [cf, cbk]
        finals += [(hf + pf * cf)[SUBLANES - 1:SUBLANES, :], (hb + pb * cbk)[0:1, :]]

    if final_only:
        for l in range(n_slab):
            o_ref[0:1, lanes[l]] = finals[2 * l]
            o_ref[1:2, lanes[l]] = finals[2 * l + 1]
        return

    def pass2(j, carry):
        jb = seg - 1 - j
        out = []
        for l in range(n_slab):
            hf, hb = carry[2 * l:2 * l + 2]
            hf = a_scr[l, tile(j), :] * hf + b_scr[l, tile(j), :]
            y_scr[l, tile(j), :] = hf
            hb = a_scr[n_slab + l, tile(jb), :] * hb + b_scr[n_slab + l, tile(jb), :]
            y_scr[n_slab + l, tile(jb), :] = hb
            out += [hf, hb]
        return tuple(out)

    lax.fori_loop(0, seg, pass2, tuple(starts), unroll=8)

    for s in range(SUBLANES):
        for l in range(n_slab):
            y = (y_scr[l, pl.ds(s, seg, stride=SUBLANES), :] + y_scr[n_slab + l, pl.ds(s, seg, stride=SUBLANES), :])
            g = g_ref[s * seg:(s + 1) * seg, lanes[l]].astype(F32)
            o_ref[s * seg:(s + 1) * seg, lanes[l]] = (_gelu_tanh(g) * y).astype(o_ref.dtype)


def _lru(p, conv_w, conv_b, wa, wx, ba, bx, lam, h0, *, batch, seq, col, final_only):
    n_gate, gate = wa.shape[1], wa.shape[2]
    w = n_gate * gate
    starts = [col["x_lru"]] + ([] if final_only else [col["g_lru"]])
    wide = LRU_CHANNELS % gate == 0 and all(c % LRU_CHANNELS == 0 for c in starts + [w])
    blk = LRU_CHANNELS if wide else gate
    nb = w // blk
    n_slab = blk // LANES
    assert seq % (2 * SUBLANES * SUBLANES) == 0 and seq % LRU_ROWS == 0
    xb = col["x_lru"] // blk
    in_specs = [
        pl.BlockSpec((seq, blk), lambda b, n: (b, xb + n)),
        pl.BlockSpec((conv_w.shape[0], blk), lambda b, n: (0, n)),
        pl.BlockSpec((1, blk), lambda b, n: (0, n)),
        pl.BlockSpec((2, blk // gate, gate, gate), lambda b, n: (0, n, 0, 0)),
        pl.BlockSpec((2, blk // gate, gate, gate), lambda b, n: (0, n, 0, 0)),
        pl.BlockSpec((2, blk), lambda b, n: (0, n)),
        pl.BlockSpec((2, blk), lambda b, n: (0, n)),
        pl.BlockSpec((2, blk), lambda b, n: (0, n)),
        pl.BlockSpec((None, 2, blk), lambda b, n: (b, 0, n)),
    ]
    args = [p, conv_w, conv_b.reshape(1, w), wa, wx, ba, bx, lam, h0]
    if final_only:
        out_spec = pl.BlockSpec((None, 2, blk), lambda b, n: (b, 0, n))
        out_shape = jax.ShapeDtypeStruct((batch, 2, w), F32)
    else:
        gb = col["g_lru"] // blk
        in_specs.append(pl.BlockSpec((seq, blk), lambda b, n: (b, gb + n)))
        args.append(p)
        out_spec = pl.BlockSpec((seq, blk), lambda b, n: (b, n))
        out_shape = jax.ShapeDtypeStruct((batch * seq, w), BF16)
    return pl.pallas_call(
        functools.partial(_lru_kernel, seq=seq, final_only=final_only),
        grid=(batch, nb),
        in_specs=in_specs,
        out_specs=out_spec,
        out_shape=out_shape,
        scratch_shapes=[
            pltpu.VMEM((n_slab, seq + SUBLANES * SUBLANES, LANES), F32),
            pltpu.VMEM((n_slab, seq + (conv_w.shape[0] - 1) * SUBLANES, LANES), F32),
        ] + [pltpu.VMEM((2 * n_slab, seq, LANES), F32)] * (2 if final_only else 3),
        compiler_params=_params(2),
        name="lru_ctx" if final_only else "lru",
    )(*args)


def _ffn_up_kernel(x_ref, wg_ref, wv_ref, cwg_ref, cwv_ref, cbg_ref, cbv_ref, o_ref):
    x = x_ref[...]
    t_idx = lax.broadcasted_iota(jnp.int32, o_ref.shape, 0)

    def conv(u, cw, cb):
        return cw[0:1] * _shift_rows(u, 1, t_idx) + cw[1:2] * u + cw[2:3] * _shift_rows(u, -1, t_idx) + cb

    g = conv(jnp.dot(x, wg_ref[...].astype(BF16), preferred_element_type=F32), cwg_ref[...], cbg_ref[...])
    v = conv(jnp.dot(x, wv_ref[...].astype(BF16), preferred_element_type=F32), cwv_ref[...], cbv_ref[...])
    o_ref[...] = (_silu(g) * v).astype(o_ref.dtype)


def _ffn_up(h, w_up, conv_w, conv_b, *, batch, seq, bn):
    d = h.shape[1]
    f = w_up.shape[1] // 2
    nj = f // bn
    cb = conv_b.reshape(1, 2 * f)
    taps = conv_w.shape[0]
    return pl.pallas_call(
        _ffn_up_kernel,
        grid=(batch, nj),
        in_specs=[
            pl.BlockSpec((seq, d), lambda b, j: (b, 0)),
            pl.BlockSpec((d, bn), lambda b, j: (0, j)),
            pl.BlockSpec((d, bn), lambda b, j: (0, nj + j)),
            pl.BlockSpec((taps, bn), lambda b, j: (0, j)),
            pl.BlockSpec((taps, bn), lambda b, j: (0, nj + j)),
            pl.BlockSpec((1, bn), lambda b, j: (0, j)),
            pl.BlockSpec((1, bn), lambda b, j: (0, nj + j)),
        ],
        out_specs=pl.BlockSpec((seq, bn), lambda b, j: (b, j)),
        out_shape=jax.ShapeDtypeStruct((batch * seq, f), BF16),
        compiler_params=_params(2),
        name="ffn_up",
    )(h, w_up, w_up, conv_w, conv_w, cb, cb)


def _tile(n, pref):
    return pref if n % pref == 0 else n


def kernel(x, c, ctx, c_ctx, w_ada, b_ada, norm1, norm2, w_in, ret_decay_logit, lru_conv_w, lru_conv_b,
           lru_wa, lru_ba, lru_wx, lru_bx, lru_lambda, w_ret_o, w_lru_o, w_out, w_up, ffn_conv_w,
           ffn_conv_b, w_down, final_norm):
    batch, seq, d = x.shape
    ctx_len = ctx.shape[1]
    assert w_in.shape[0] == 1, "single-layer trunk"
    assert batch + 1 <= ADA_ROWS
    n_heads = ret_decay_logit.shape[-1]
    qk_w, v_w, lru_w = n_heads * RET_QK_DIM, n_heads * RET_V_DIM, lru_lambda.shape[-1]
    sizes = (("k", qk_w), ("v", v_w), ("x_lru", lru_w), ("q", qk_w), ("g_ret", v_w), ("g_lru", lru_w),
             ("m_ret", d), ("m_lru", d))
    col, off = {}, 0
    for name, size in sizes:
        col[name] = off
        off += size
    state_cols = col["q"]
    m = batch * seq
    f = w_down.shape[1]

    bf = lambda a: a.astype(BF16)
    w_down_b = bf(w_down[0])
    wa_b, wx_b = bf(0.5 * lru_wa[0]), bf(0.5 * lru_wx[0])

    c_rows = jnp.zeros((ADA_ROWS, d), F32).at[:batch].set(c).at[batch].set(c_ctx)
    mod = _ada(c_rows, w_ada[0], b_ada[0], tn=_tile(N_MOD * d, ADA_COLS))
    sh1, sc1, g1, sh2, sc2, g2 = [mod[:batch, i * d:(i + 1) * d].reshape(batch, 1, d) for i in range(N_MOD)]
    csh, csc = [mod[batch, i * d:(i + 1) * d].reshape(1, 1, d) for i in range(2)]

    h_ctx = _norm_mod(ctx, norm1[0], csh, csc, tm=_tile(ctx_len, NORM_ROWS))
    pc = _matmul(h_ctx, w_in[0], bm=_tile(batch * ctx_len, ROWS), bn=_tile(state_cols, COLS), n_cols=state_cols,
                 out_dtype=BF16, name="w_in_ctx")
    lru_args = (lru_conv_w[0], lru_conv_b[0], wa_b, wx_b, 0.5 * lru_ba[0], 0.5 * lru_bx[0], lru_lambda[0])
    h0 = _lru(pc, *lru_args, jnp.zeros((batch, 2, lru_w), F32), batch=batch, seq=ctx_len, col=col,
              final_only=True)

    h_lat = _norm_mod(x, norm1[0], sh1, sc1, tm=_tile(seq, NORM_ROWS))
    p = _matmul(h_lat, w_in[0], bm=_tile(seq, ROWS_WIDE), bn=_tile(off, COLS), out_dtype=BF16, name="w_in")
    cos2, sin2 = _rope_tables(seq)
    a_ret = _retention(p, pc, cos2, sin2, ret_decay_logit[0], batch=batch, seq=seq, ctx_len=ctx_len,
                       n_heads=n_heads, col=col)
    a_lru = _lru(p, *lru_args, h0, batch=batch, seq=seq, col=col, final_only=False)
    bm, bnd = _tile(seq, ROWS), _tile(d, COLS)
    y = _merge(a_ret, w_ret_o[0], a_lru, w_lru_o[0], p, col["m_ret"], col["m_lru"], bm=bm,
               bn=_tile(d, COLS_NARROW))
    x_lat = _out_proj(y, w_out[0], x.reshape(m, d), g1, seq=seq, bm=bm, bn=bnd)

    h2 = _norm_mod(x_lat.reshape(batch, seq, d), norm2[0], sh2, sc2, tm=_tile(seq, NORM_ROWS))
    act = _ffn_up(h2, w_up[0], ffn_conv_w[0], ffn_conv_b[0], batch=batch, seq=seq, bn=_tile(f, COLS_NARROW))
    out = _down_rms(act, w_down_b, x_lat, g2, final_norm, seq=seq, bm=_tile(seq, ROWS_RESIDENT),
                    bn=_tile(d, COLS_NARROW))
    return out.reshape(batch, seq, d)
```
